```python
import math
import jax, jax.numpy as jnp
from jax import lax
import numpy as np

D_MODEL = 2048
BATCH = 1
SEQ = 8192
DEPTH = 2

GRID_W = 64
NORM_EPS = 1e-6
RWKV_HEAD = 64
RWKV_HEADS = D_MODEL // RWKV_HEAD
DECAY_LORA = 96
ICLR_LORA = 96
GATE_LORA = 256
GN_EPS = 64e-5
ATT_HEAD = 128
ATT_Q_HEADS = D_MODEL // ATT_HEAD
ATT_KV_HEADS = 4
Q_BLOCK = 128
ROPE_THETA = 10000.0
D_FF = 7 * D_MODEL // 2
N_EXPERTS = 8
TOP_K = 2

kernel_name = 'hybrid_rwkv7_axial_gqa_moe_encoder'


def rms_norm(x, g):
    xf = x.astype(jnp.float32)
    y = xf * lax.rsqrt(jnp.mean(xf * xf, axis=-1, keepdims=True) + NORM_EPS)
    return (y * g.astype(jnp.float32)).astype(x.dtype)


def centred_shift(x):
    prev = jnp.pad(x[:, :-1], ((0, 0), (1, 0), (0, 0)))
    nxt = jnp.pad(x[:, 1:], ((0, 0), (0, 1), (0, 0)))
    return 0.5 * (prev + nxt) - x


def wkv7_scan(r, w, k, v, a, b, reverse):
    B, T, H, N = r.shape

    def step(S, inp):
        r_t, w_t, k_t, v_t, a_t, b_t = inp
        sa = jnp.einsum('bhvk,bhk->bhv', S, a_t)
        S = S * w_t[:, :, None, :] + sa[..., None] * b_t[:, :, None, :] + v_t[..., None] * k_t[:, :, None, :]
        return S, jnp.einsum('bhvk,bhk->bhv', S, r_t)

    S0 = jnp.zeros((B, H, N, N), jnp.float32)
    xs = tuple(jnp.moveaxis(t, 1, 0) for t in (r, w, k, v, a, b))
    _, ys = lax.scan(step, S0, xs, reverse=reverse)
    return jnp.moveaxis(ys, 0, 1)


def rwkv7_time_mix(x, mix, w_rkv, w0, w1, w2, a0, a1, a2, g1, g2, k_k, k_a, r_k, ln_w, ln_b, w_out):
    B, T, D = x.shape
    H, N = RWKV_HEADS, RWKV_HEAD
    f32 = jnp.float32
    heads = lambda t: t.reshape(B, T, H, N)
    xx = centred_shift(x)
    xr = x + xx * mix[0]
    xw = x + xx * mix[1]
    xk = x + xx * mix[2]
    xv = x + xx * mix[3]
    xa = x + xx * mix[4]
    xg = x + xx * mix[5]
    rkv = jnp.einsum('nbtd,nde->nbte', jnp.stack([xr, xk, xv]), w_rkv)
    r, k, v = rkv[0], rkv[1], rkv[2]
    g = jax.nn.sigmoid(xg @ g1) @ g2
    kf = k.astype(f32)
    kk = heads(kf * k_k.astype(f32))
    kk = kk / jnp.maximum(jnp.sqrt(jnp.sum(kk * kk, axis=-1, keepdims=True)), 1e-12)
    rh = heads(r.astype(f32))
    vh = heads(v.astype(f32))
    r_kf = r_k.astype(f32)
    ys = []
    bonuses = []
    for d in range(2):
        wl = (w0[d] + jnp.tanh(xw @ w1[d]) @ w2[d]).astype(f32)
        decay = jnp.exp(-jnp.exp(-jax.nn.softplus(-wl) - 0.5))
        a = jax.nn.sigmoid((a0[d] + (xa @ a1[d]) @ a2[d]).astype(f32))
        kd = heads(kf * (1.0 + (a - 1.0) * k_a.astype(f32)))
        ah = heads(a)
        ys.append(wkv7_scan(rh, heads(decay), kd, vh, -kk, kk * ah, reverse=(d == 1)))
        bonuses.append(jnp.sum(rh * kd * r_kf, axis=-1, keepdims=True) * vh)
    y = ys[0] + ys[1]
    mu = jnp.mean(y, axis=-1, keepdims=True)
    var = jnp.mean(jnp.square(y - mu), axis=-1, keepdims=True)
    y = (y - mu) * lax.rsqrt(var + GN_EPS)
    y = y.reshape(B, T, D) * ln_w.astype(f32) + ln_b.astype(f32) + (bonuses[0] + bonuses[1]).reshape(B, T, D)
    return (y * g.astype(f32)).astype(x.dtype) @ w_out


def axial_rope_tables(T):
    rows = T // GRID_W
    row_ids = jnp.repeat(jnp.arange(rows), GRID_W).astype(jnp.float32)
    col_ids = jnp.tile(jnp.arange(GRID_W), rows).astype(jnp.float32)
    axis_dim = ATT_HEAD // 2
    inv = ROPE_THETA ** (-jnp.arange(0, axis_dim, 2, dtype=jnp.float32) / axis_dim)
    ang_r = row_ids[:, None] * inv[None, :]
    ang_c = col_ids[:, None] * inv[None, :]
    return jnp.cos(ang_r), jnp.sin(ang_r), jnp.cos(ang_c), jnp.sin(ang_c)


def rope_axis(u, cos, sin):
    u1, u2 = jnp.split(u, 2, axis=-1)
    c = cos[None, :, None, :]
    s = sin[None, :, None, :]
    return jnp.concatenate([u1 * c - u2 * s, u2 * c + u1 * s], axis=-1)


def apply_axial_rope(u, tabs):
    cr, sr, cc, sc = tabs
    uf = u.astype(jnp.float32)
    half = u.shape[-1] // 2
    out = jnp.concatenate([rope_axis(uf[..., :half], cr, sr), rope_axis(uf[..., half:], cc, sc)], axis=-1)
    return out.astype(u.dtype)


def gqa_axial_attention(x, w_qkv, q_norm, k_norm, w_o):
    B, T, D = x.shape
    QH, KH, HD = ATT_Q_HEADS, ATT_KV_HEADS, ATT_HEAD
    rep = QH // KH
    qkv = x @ w_qkv
    q, k, v = jnp.split(qkv, [QH * HD, (QH + KH) * HD], axis=-1)
    q = rms_norm(q.reshape(B, T, QH, HD), q_norm)
    k = rms_norm(k.reshape(B, T, KH, HD), k_norm)
    v = v.reshape(B, T, KH, HD)
    tabs = axial_rope_tables(T)
    q = apply_axial_rope(q, tabs)
    k = apply_axial_rope(k, tabs)
    scale = 1.0 / math.sqrt(HD)
    nblk = T // Q_BLOCK
    qb = q.reshape(B, nblk, Q_BLOCK, KH, rep, HD).transpose(1, 0, 2, 3, 4, 5)

    def block(q_blk):
        s = jnp.einsum('bqgrd,bkgd->bgrqk', q_blk, k).astype(jnp.float32) * scale
        p = jax.nn.softmax(s, axis=-1).astype(v.dtype)
        return jnp.einsum('bgrqk,bkgd->bqgrd', p, v)

    o = lax.map(block, qb)
    o = o.transpose(1, 0, 2, 3, 4, 5).reshape(B, T, QH * HD)
    return o @ w_o


def swiglu(x, w_gate, w_up, w_down):
    return (jax.nn.silu(x @ w_gate) * (x @ w_up)) @ w_down


def moe_swiglu(x, w_router, w_gate, w_up, w_down):
    logits = (x @ w_router).astype(jnp.float32)
    top_v, top_i = lax.top_k(logits, TOP_K)
    top_p = jax.nn.softmax(top_v, axis=-1)
    gates = jnp.sum(jax.nn.one_hot(top_i, N_EXPERTS, dtype=jnp.float32) * top_p[..., None], axis=-2)
    y = jnp.zeros_like(x)
    for e in range(N_EXPERTS):
        y = y + gates[..., e:e + 1].astype(x.dtype) * swiglu(x, w_gate[e], w_up[e], w_down[e])
    return y


def setup_inputs(seed: int = 0) -> dict:
    key = jax.random.key(seed)
    ks = iter(jax.random.split(key, 48))
    D = D_MODEL
    na = (DEPTH + 1) // 2
    nb = DEPTH // 2
    H, N = RWKV_HEADS, RWKV_HEAD
    qkv_w = (ATT_Q_HEADS + 2 * ATT_KV_HEADS) * ATT_HEAD

    def nrm(shape, scale):
        return jax.random.normal(next(ks), shape, jnp.float32) * scale

    def uni(shape):
        return jax.random.uniform(next(ks), shape, jnp.float32)

    w0_base = jnp.broadcast_to(jnp.linspace(-6.5, -1.5, D, dtype=jnp.float32), (na, 2, D))
    return {
        'x': nrm((BATCH, SEQ, D), 1.0),
        'norm_mix_g': 1.0 + nrm((DEPTH, D), 0.05),
        'norm_ffn_g': 1.0 + nrm((DEPTH, D), 0.05),
        'rwkv_mix': uni((na, 6, D)),
        'rwkv_w_rkv': nrm((na, 3, D, D), D ** -0.5),
        'rwkv_w0': w0_base + nrm((na, 2, D), 0.1),
        'rwkv_w1': nrm((na, 2, D, DECAY_LORA), D ** -0.5),
        'rwkv_w2': nrm((na, 2, DECAY_LORA, D), 0.3 * DECAY_LORA ** -0.5),
        'rwkv_a0': nrm((na, 2, D), 0.1),
        'rwkv_a1': nrm((na, 2, D, ICLR_LORA), D ** -0.5),
        'rwkv_a2': nrm((na, 2, ICLR_LORA, D), 0.5 * ICLR_LORA ** -0.5),
        'rwkv_g1': nrm((na, D, GATE_LORA), D ** -0.5),
        'rwkv_g2': nrm((na, GATE_LORA, D), GATE_LORA ** -0.5),
        'rwkv_k_k': 0.85 + nrm((na, D), 0.05),
        'rwkv_k_a': 1.0 + nrm((na, D), 0.05),
        'rwkv_r_k': nrm((na, H, N), 0.1),
        'rwkv_ln_w': 1.0 + nrm((na, D), 0.05),
        'rwkv_ln_b': nrm((na, D), 0.02),
        'rwkv_w_out': nrm((na, D, D), D ** -0.5),
        'attn_w_qkv': nrm((nb, D, qkv_w), D ** -0.5),
        'attn_q_norm': 1.0 + nrm((nb, ATT_HEAD), 0.05),
        'attn_k_norm': 1.0 + nrm((nb, ATT_HEAD), 0.05),
        'attn_w_o': nrm((nb, ATT_Q_HEADS * ATT_HEAD, D), D ** -0.5),
        'ffn_w_gate': nrm((na, D, D_FF), D ** -0.5),
        'ffn_w_up': nrm((na, D, D_FF), D ** -0.5),
        'ffn_w_down': nrm((na, D_FF, D), D_FF ** -0.5),
        'moe_router': nrm((nb, D, N_EXPERTS), D ** -0.5),
        'moe_w_gate': nrm((nb, N_EXPERTS, D, D_FF), D ** -0.5),
        'moe_w_up': nrm((nb, N_EXPERTS, D, D_FF), D ** -0.5),
        'moe_w_down': nrm((nb, N_EXPERTS, D_FF, D), D_FF ** -0.5),
        'final_norm_g': 1.0 + nrm((D,), 0.05),
    }


def reference(x, norm_mix_g, norm_ffn_g, rwkv_mix, rwkv_w_rkv, rwkv_w0, rwkv_w1, rwkv_w2, rwkv_a0, rwkv_a1, rwkv_a2, rwkv_g1, rwkv_g2, rwkv_k_k, rwkv_k_a, rwkv_r_k, rwkv_ln_w, rwkv_ln_b, rwkv_w_out, attn_w_qkv, attn_q_norm, attn_k_norm, attn_w_o, ffn_w_gate, ffn_w_up, ffn_w_down, moe_router, moe_w_gate, moe_w_up, moe_w_down, final_norm_g):
    h = x
    for i in range(DEPTH):
        j = i // 2
        u = rms_norm(h, norm_mix_g[i])
        if i % 2 == 0:
            h = h + rwkv7_time_mix(u, rwkv_mix[j], rwkv_w_rkv[j], rwkv_w0[j], rwkv_w1[j], rwkv_w2[j], rwkv_a0[j], rwkv_a1[j], rwkv_a2[j], rwkv_g1[j], rwkv_g2[j], rwkv_k_k[j], rwkv_k_a[j], rwkv_r_k[j], rwkv_ln_w[j], rwkv_ln_b[j], rwkv_w_out[j])
        else:
            h = h + gqa_axial_attention(u, attn_w_qkv[j], attn_q_norm[j], attn_k_norm[j], attn_w_o[j])
        u = rms_norm(h, norm_ffn_g[i])
        if i % 2 == 0:
            h = h + swiglu(u, ffn_w_gate[j], ffn_w_up[j], ffn_w_down[j])
        else:
            h = h + moe_swiglu(u, moe_router[j], moe_w_gate[j], moe_w_up[j], moe_w_down[j])
    return rms_norm(h, final_norm_g)
```

```python
import functools
import math

import jax
import jax.numpy as jnp
from jax import lax
from jax.experimental import pallas as pl
from jax.experimental.pallas import tpu as pltpu

F32 = jnp.float32
BF16 = jnp.bfloat16
HIGHEST = lax.Precision.HIGHEST

LANES = 128
RWKV_HEAD = 64
WKV_CHUNK = 64
GN_EPS = 64e-5
NORM_EPS = 1e-6


def _dot(a, b):
    return jnp.dot(a.astype(BF16), b.astype(BF16), preferred_element_type=F32)


def _dot_nt(a, b):
    return lax.dot_general(a.astype(BF16), b.astype(BF16), (((1,), (1,)), ((), ())),
                           preferred_element_type=F32)


def _dot_tn(a, b):
    return lax.dot_general(a.astype(BF16), b.astype(BF16), (((0,), (0,)), ((), ())),
                           preferred_element_type=F32)


def _dot_f32(a, b):
    return jnp.dot(a, b, preferred_element_type=F32, precision=HIGHEST)


def _wkv_kernel(r_ref, k_ref, v_ref, lw_ref, a_ref, kk_ref, ka_ref, rk_ref,
                y_ref, bonus_ref, s_ref, *, pairs):
    L = WKV_CHUNK
    d = pl.program_id(0)
    c = pl.program_id(2)

    @pl.when(c == 0)
    def _():
        s_ref[...] = jnp.zeros_like(s_ref)

    sign = 1 - 2 * d
    row = lax.broadcasted_iota(jnp.int32, (L, 2 * L), 0)
    col = lax.broadcasted_iota(jnp.int32, (L, 2 * L), 1)
    order = (row - (col & (L - 1))) * sign
    strict = order > 0
    incl = order >= 0
    first_head_cols = col < L
    r2 = lax.broadcasted_iota(jnp.int32, (L, L), 0)
    c2 = lax.broadcasted_iota(jnp.int32, (L, L), 1)
    tri = jnp.where((r2 - c2) * sign >= 0, 1.0, 0.0).astype(F32)
    lane = lax.broadcasted_iota(jnp.int32, (L, LANES), 1)
    first_head = lane < RWKV_HEAD
    hr = lax.broadcasted_iota(jnp.int32, (LANES, LANES), 0)
    hc = lax.broadcasted_iota(jnp.int32, (LANES, LANES), 1)
    head_ones = jnp.where((hr < RWKV_HEAD) == (hc < RWKV_HEAD), 1.0, 0.0).astype(F32)

    def stack(x):
        return jnp.concatenate([jnp.where(first_head, x, 0.0), jnp.where(first_head, 0.0, x)], axis=0)

    def block_diag(m):
        return jnp.concatenate([jnp.where(first_head_cols, m, 0.0), jnp.where(first_head_cols, 0.0, m)], axis=0)

    for p in range(pairs):
        sl = slice(p * LANES, (p + 1) * LANES)
        r = r_ref[:, sl]
        k = k_ref[:, sl]
        v = v_ref[:, sl]
        lw = lw_ref[0, :, sl]
        a = a_ref[0, :, sl]
        k_k = kk_ref[:, sl]
        k_a = ka_ref[:, sl]
        r_k = rk_ref[:, sl]

        kkr = k * k_k
        ss = _dot_f32(kkr * kkr, head_ones)
        kk = kkr / jnp.maximum(jnp.sqrt(ss), 1e-12)
        kd = k * (1.0 + (a - 1.0) * k_a)
        b = kk * a
        bonus_ref[0, :, sl] = _dot_f32(r * kd * r_k, head_ones) * v

        cs = _dot_f32(tri, lw)
        tot = jnp.sum(lw, axis=0, keepdims=True)
        a_t = -kk * jnp.exp(cs - lw)
        r_t = r * jnp.exp(cs)
        g_inv = jnp.exp(-cs)
        b_t = b * g_inv
        k_t = kd * g_inv
        g_rem = jnp.exp(tot - cs)
        b_h = b * g_rem
        k_h = kd * g_rem

        sc = _dot_nt(jnp.concatenate([a_t, r_t], axis=0),
                     jnp.concatenate([stack(b_t), stack(k_t)], axis=0))
        m_ab = jnp.where(strict, sc[:L, :2 * L], 0.0)
        m_ak = jnp.where(strict, sc[:L, 2 * L:], 0.0)
        m_rb = jnp.where(incl, sc[L:, :2 * L], 0.0)
        m_rk = jnp.where(incl, sc[L:, 2 * L:], 0.0)

        pw = block_diag(m_ab)
        n = pw
        for _ in range(int(math.log2(L)) - 1):
            pw = _dot(pw, pw)
            n = n + pw + _dot(n, pw)

        s = s_ref[p]
        vs = stack(v)
        x = _dot_nt(a_t, s) + _dot(m_ak, vs)
        xs = stack(x)
        us = xs + _dot(n, xs)
        y_ref[0, :, sl] = _dot_nt(r_t, s) + _dot(m_rb, us) + _dot(m_rk, vs)
        s_ref[p] = s * jnp.exp(tot) + _dot_tn(jnp.concatenate([us, vs], axis=0),
                                               jnp.concatenate([stack(b_h), stack(k_h)], axis=0))


def wkv7_bidir(r, k, v, logw, a, k_k, k_a, r_k, *, pairs=2, interpret=False):
    T, D = r.shape
    L = WKV_CHUNK
    nc = T // L
    W = LANES * pairs
    nj = D // W
    row_map = lambda d, j, c: (c + d * (nc - 1 - 2 * c), j)
    dir_map = lambda d, j, c: (d, c + d * (nc - 1 - 2 * c), j)
    par_map = lambda d, j, c: (0, j)
    tok = pl.BlockSpec((L, W), row_map)
    dtok = pl.BlockSpec((1, L, W), dir_map)
    par = pl.BlockSpec((1, W), par_map)
    return pl.pallas_call(
        functools.partial(_wkv_kernel, pairs=pairs),
        grid=(2, nj, nc),
        in_specs=[tok, tok, tok, dtok, dtok, par, par, par],
        out_specs=[dtok, dtok],
        out_shape=[jax.ShapeDtypeStruct((2, T, D), F32)] * 2,
        scratch_shapes=[pltpu.VMEM((pairs, LANES, LANES), F32)],
        compiler_params=pltpu.CompilerParams(dimension_semantics=("arbitrary", "arbitrary", "arbitrary")),
        name="wkv7_bidir",
        interpret=interpret,
    )(r, k, v, logw, a, k_k.reshape(1, D), k_a.reshape(1, D), r_k.reshape(1, D))


def _params(sem, vmem_mb):
    return pltpu.CompilerParams(dimension_semantics=sem, vmem_limit_bytes=vmem_mb * 1024 * 1024)


def _rms(x, g):
    return x * lax.rsqrt(jnp.mean(x * x, axis=-1, keepdims=True) + NORM_EPS) * g


def _norm_kernel(h_ref, g_ref, o_ref):
    o_ref[...] = _rms(h_ref[...], g_ref[...]).astype(o_ref.dtype)


def rms_norm_call(h, g, out_dtype, *, tm=256, interpret=False):
    T, D = h.shape
    tm = min(tm, T)
    return pl.pallas_call(
        _norm_kernel,
        grid=(T // tm,),
        in_specs=[pl.BlockSpec((tm, D), lambda i: (i, 0)), pl.BlockSpec((1, D), lambda i: (0, 0))],
        out_specs=pl.BlockSpec((tm, D), lambda i: (i, 0)),
        out_shape=jax.ShapeDtypeStruct((T, D), out_dtype),
        compiler_params=_params(("arbitrary",), 32),
        name="rms_norm",
        interpret=interpret,
    )(h, g.reshape(1, D))


HALO = 8


def _mixprep_kernel(h_ref, hp_ref, hn_ref, g_ref, mix_ref, o_ref):
    i = pl.program_id(0)
    last = pl.num_programs(0) - 1
    g = g_ref[...]
    u = _rms(h_ref[...], g)
    tm = u.shape[0]
    up = jnp.where(i == 0, 0.0, _rms(hp_ref[HALO - 1:HALO, :], g))
    un = jnp.where(i == last, 0.0, _rms(hn_ref[0:1, :], g))
    rows = lax.broadcasted_iota(jnp.int32, u.shape, 0)
    prev = jnp.where(rows == 0, up, pltpu.roll(u, 1, 0))
    nxt = jnp.where(rows == tm - 1, un, pltpu.roll(u, tm - 1, 0))
    xx = 0.5 * (prev + nxt) - u
    for m in range(o_ref.shape[0]):
        o_ref[m] = (u + xx * mix_ref[m:m + 1, :]).astype(o_ref.dtype)


def mixprep_call(h, g, mix, *, tm=256, interpret=False):
    T, D = h.shape
    tm = min(tm, T)
    nm = mix.shape[0]
    nb = tm // HALO
    nh = T // HALO
    return pl.pallas_call(
        _mixprep_kernel,
        grid=(T // tm,),
        in_specs=[pl.BlockSpec((tm, D), lambda i: (i, 0)),
                  pl.BlockSpec((HALO, D), lambda i: (jnp.maximum(i * nb - 1, 0), 0)),
                  pl.BlockSpec((HALO, D), lambda i: (jnp.minimum((i + 1) * nb, nh - 1), 0)),
                  pl.BlockSpec((1, D), lambda i: (0, 0)),
                  pl.BlockSpec((nm, D), lambda i: (0, 0))],
        out_specs=pl.BlockSpec((nm, tm, D), lambda i: (0, i, 0)),
        out_shape=jax.ShapeDtypeStruct((nm, T, D), BF16),
        compiler_params=_params(("arbitrary",), 48),
        name="rwkv_mixprep",
        interpret=interpret,
    )(h, h, h, g.reshape(1, D), mix)


def _mm_kernel(*refs, has_resid):
    if has_resid:
        x_ref, w_ref, r_ref, o_ref, wb_ref = refs
    else:
        x_ref, w_ref, o_ref, wb_ref = refs

    @pl.when(pl.program_id(2) == 0)
    def _():
        wb_ref[...] = w_ref[...].astype(BF16)

    acc = jnp.dot(x_ref[...], wb_ref[...], preferred_element_type=F32)
    if has_resid:
        acc = acc + r_ref[...]
    o_ref[...] = acc.astype(o_ref.dtype)


def matmul_call(x, w, resid=None, *, out_dtype=F32, tm=512, tn=512, name="matmul", interpret=False):
    _, T, Kd = x.shape
    S, _, N = w.shape
    tm, tn = min(tm, T), min(tn, N)
    in_specs = [pl.BlockSpec((None, tm, Kd), lambda s, j, i: (s, i, 0)),
                pl.BlockSpec((None, Kd, tn), lambda s, j, i: (s, 0, j))]
    args = [x, w]
    if resid is not None:
        in_specs.append(pl.BlockSpec((None, tm, tn), lambda s, j, i: (s, i, j)))
        args.append(resid)
    return pl.pallas_call(
        functools.partial(_mm_kernel, has_resid=resid is not None),
        grid=(S, N // tn, T // tm),
        in_specs=in_specs,
        out_specs=pl.BlockSpec((None, tm, tn), lambda s, j, i: (s, i, j)),
        out_shape=jax.ShapeDtypeStruct((S, T, N), out_dtype),
        scratch_shapes=[pltpu.VMEM((Kd, tn), BF16)],
        compiler_params=_params(("arbitrary", "arbitrary", "arbitrary"), 48),
        name=name,
        interpret=interpret,
    )(*args)


def _lora_kernel(xw_ref, xa_ref, xg_ref, w1_ref, w2_ref, w0_ref, a1_ref, a2_ref, a0_ref, g1_ref, g2_ref,
                 g_ref, lw_ref, a_ref):
    xw = xw_ref[...]
    xa = xa_ref[...]
    g_ref[...] = _dot(jax.nn.sigmoid(_dot(xg_ref[...], g1_ref[...])), g2_ref[...])
    for d in range(2):
        wl = w0_ref[d:d + 1, :] + _dot(jnp.tanh(_dot(xw, w1_ref[d])), w2_ref[d])
        lw_ref[d] = -jnp.exp(-jax.nn.softplus(-wl) - 0.5)
        a_ref[d] = jax.nn.sigmoid(a0_ref[d:d + 1, :] + _dot(_dot(xa, a1_ref[d]), a2_ref[d]))


def _pad_to(x, axis, mult):
    pad = (-x.shape[axis]) % mult
    if pad == 0:
        return x
    widths = [(0, 0)] * x.ndim
    widths[axis] = (0, pad)
    return jnp.pad(x, widths)


def lora_call(xs, w0, w1, w2, a0, a1, a2, g1, g2, *, tm=256, interpret=False):
    _, T, D = xs.shape
    tm = min(tm, T)
    w1 = _pad_to(w1, 2, LANES).astype(BF16)
    w2 = _pad_to(w2, 1, LANES).astype(BF16)
    a1 = _pad_to(a1, 2, LANES).astype(BF16)
    a2 = _pad_to(a2, 1, LANES).astype(BF16)
    g1 = g1.astype(BF16)
    g2 = g2.astype(BF16)
    xspec = lambda m: pl.BlockSpec((None, tm, D), lambda i: (m, i, 0))
    full = lambda a: pl.BlockSpec(a.shape, lambda i: (0,) * a.ndim)
    return pl.pallas_call(
        _lora_kernel,
        grid=(T // tm,),
        in_specs=[xspec(3), xspec(4), xspec(5), full(w1), full(w2), full(w0), full(a1), full(a2), full(a0),
                  full(g1), full(g2)],
        out_specs=[pl.BlockSpec((tm, D), lambda i: (i, 0)),
                   pl.BlockSpec((2, tm, D), lambda i: (0, i, 0)),
                   pl.BlockSpec((2, tm, D), lambda i: (0, i, 0))],
        out_shape=[jax.ShapeDtypeStruct((T, D), F32), jax.ShapeDtypeStruct((2, T, D), F32),
                   jax.ShapeDtypeStruct((2, T, D), F32)],
        compiler_params=_params(("arbitrary",), 56),
        name="rwkv_lora",
        interpret=interpret,
    )(xs, xs, xs, w1, w2, w0, a1, a2, a0, g1, g2)


def _wkv_post_kernel(y_ref, b_ref, g_ref, lnw_ref, lnb_ref, o_ref):
    hr = lax.broadcasted_iota(jnp.int32, (LANES, LANES), 0)
    hc = lax.broadcasted_iota(jnp.int32, (LANES, LANES), 1)
    head_mean = jnp.where((hr < RWKV_HEAD) == (hc < RWKV_HEAD), 1.0 / RWKV_HEAD, 0.0).astype(F32)
    for p in range(o_ref.shape[1] // LANES):
        sl = slice(p * LANES, (p + 1) * LANES)
        y = y_ref[0, :, sl] + y_ref[1, :, sl]
        yc = y - _dot_f32(y, head_mean)
        var = _dot_f32(yc * yc, head_mean)
        yn = yc * lax.rsqrt(var + GN_EPS)
        out = yn * lnw_ref[:, sl] + lnb_ref[:, sl] + b_ref[0, :, sl] + b_ref[1, :, sl]
        o_ref[:, sl] = (out * g_ref[:, sl]).astype(o_ref.dtype)


def wkv_post_call(y, bonus, g, ln_w, ln_b, *, tm=256, interpret=False):
    _, T, D = y.shape
    tm = min(tm, T)
    two = pl.BlockSpec((2, tm, D), lambda i: (0, i, 0))
    one = pl.BlockSpec((tm, D), lambda i: (i, 0))
    par = pl.BlockSpec((1, D), lambda i: (0, 0))
    return pl.pallas_call(
        _wkv_post_kernel,
        grid=(T // tm,),
        in_specs=[two, two, one, par, par],
        out_specs=one,
        out_shape=jax.ShapeDtypeStruct((T, D), BF16),
        compiler_params=_params(("arbitrary",), 48),
        name="wkv_post",
        interpret=interpret,
    )(y, bonus, g, ln_w.reshape(1, D), ln_b.reshape(1, D))


def _swiglu_kernel(*refs, gated):
    if gated:
        x_ref, wg_ref, wu_ref, wd_ref, r_ref, gate_ref, o_ref = refs
    else:
        x_ref, wg_ref, wu_ref, wd_ref, r_ref, o_ref = refs
    e = pl.program_id(1)
    f = pl.program_id(2)

    @pl.when((e == 0) & (f == 0))
    def _():
        o_ref[...] = r_ref[...]

    x = x_ref[...]
    hg = jnp.dot(x, wg_ref[...].astype(BF16), preferred_element_type=F32)
    hu = jnp.dot(x, wu_ref[...].astype(BF16), preferred_element_type=F32)
    h = hg * jax.nn.sigmoid(hg) * hu
    if gated:
        gates = gate_ref[...]
        lane = lax.broadcasted_iota(jnp.int32, gates.shape, 1)
        h = h * jnp.sum(jnp.where(lane == e, gates, 0.0), axis=-1, keepdims=True)
    o_ref[...] += jnp.dot(h.astype(BF16), wd_ref[...].astype(BF16), preferred_element_type=F32)


def swiglu_call(x, wg, wu, wd, resid, gates=None, *, tm=512, tf=512, interpret=False):
    T, D = x.shape
    E, _, Fd = wg.shape
    tm, tf = min(tm, T), min(tf, Fd)
    in_specs = [pl.BlockSpec((tm, D), lambda i, e, f: (i, 0)),
                pl.BlockSpec((None, D, tf), lambda i, e, f: (e, 0, f)),
                pl.BlockSpec((None, D, tf), lambda i, e, f: (e, 0, f)),
                pl.BlockSpec((None, tf, D), lambda i, e, f: (e, f, 0)),
                pl.BlockSpec((tm, D), lambda i, e, f: (i, 0))]
    args = [x, wg, wu, wd, resid]
    if gates is not None:
        in_specs.append(pl.BlockSpec((tm, LANES), lambda i, e, f: (i, 0)))
        args.append(gates)
    return pl.pallas_call(
        functools.partial(_swiglu_kernel, gated=gates is not None),
        grid=(T // tm, E, Fd // tf),
        in_specs=in_specs,
        out_specs=pl.BlockSpec((tm, D), lambda i, e, f: (i, 0)),
        out_shape=jax.ShapeDtypeStruct((T, D), F32),
        compiler_params=_params(("arbitrary", "arbitrary", "arbitrary"), 56),
        name="swiglu_gated" if gates is not None else "swiglu",
        interpret=interpret,
    )(*args)


ATT_HEAD = 128
ROPE_HALF = 32


def _qkprep_kernel(qkv_ref, qn_ref, kn_ref, cos_ref, sin_ref, q_ref, k_ref, v_ref, *, q_heads, kv_heads):
    cos = cos_ref[...]
    sin = sin_ref[...]
    lane = lax.broadcasted_iota(jnp.int32, cos.shape, 1)
    low = (lane & (2 * ROPE_HALF - 1)) < ROPE_HALF
    scale = 1.0 / math.sqrt(ATT_HEAD)

    def head(x, g):
        xn = _rms(x, g)
        partner = jnp.where(low, pltpu.roll(xn, ATT_HEAD - ROPE_HALF, 1), pltpu.roll(xn, ROPE_HALF, 1))
        return xn * cos + partner * sin

    for hh in range(q_heads):
        sl = slice(hh * ATT_HEAD, (hh + 1) * ATT_HEAD)
        q_ref[:, sl] = (head(qkv_ref[:, sl], qn_ref[...]) * scale).astype(q_ref.dtype)
    for hh in range(kv_heads):
        src = slice((q_heads + hh) * ATT_HEAD, (q_heads + hh + 1) * ATT_HEAD)
        dst = slice(hh * ATT_HEAD, (hh + 1) * ATT_HEAD)
        k_ref[:, dst] = head(qkv_ref[:, src], kn_ref[...]).astype(k_ref.dtype)
    v0 = (q_heads + kv_heads) * ATT_HEAD
    v_ref[...] = qkv_ref[:, v0:].astype(v_ref.dtype)


def qkprep_call(qkv, q_norm, k_norm, cos, sin, q_heads, kv_heads, *, tm=256, interpret=False):
    T, Wd = qkv.shape
    tm = min(tm, T)
    qw, kw = q_heads * ATT_HEAD, kv_heads * ATT_HEAD
    row = lambda w: pl.BlockSpec((tm, w), lambda i: (i, 0))
    par = pl.BlockSpec((1, ATT_HEAD), lambda i: (0, 0))
    return pl.pallas_call(
        functools.partial(_qkprep_kernel, q_heads=q_heads, kv_heads=kv_heads),
        grid=(T // tm,),
        in_specs=[row(Wd), par, par, row(ATT_HEAD), row(ATT_HEAD)],
        out_specs=[row(qw), row(kw), row(kw)],
        out_shape=[jax.ShapeDtypeStruct((T, qw), BF16), jax.ShapeDtypeStruct((T, kw), BF16),
                   jax.ShapeDtypeStruct((T, kw), BF16)],
        compiler_params=_params(("arbitrary",), 32),
        name="attn_qkprep",
        interpret=interpret,
    )(qkv, q_norm.reshape(1, ATT_HEAD), k_norm.reshape(1, ATT_HEAD), cos, sin)


def _rope_tables(T, grid_w):
    pos = jnp.arange(T)
    row_ids = (pos // grid_w).astype(F32)
    col_ids = (pos % grid_w).astype(F32)
    axis_dim = ATT_HEAD // 2
    inv = 10000.0 ** (-jnp.arange(0, axis_dim, 2, dtype=F32) / axis_dim)
    ang_r = row_ids[:, None] * inv[None, :]
    ang_c = col_ids[:, None] * inv[None, :]
    cos = jnp.concatenate([jnp.cos(ang_r), jnp.cos(ang_r), jnp.cos(ang_c), jnp.cos(ang_c)], axis=-1)
    sin = jnp.concatenate([-jnp.sin(ang_r), jnp.sin(ang_r), -jnp.sin(ang_c), jnp.sin(ang_c)], axis=-1)
    return cos, sin


def _flash_kernel(q_ref, k_ref, v_ref, o_ref, m_ref, l_ref, acc_ref, *, rep):
    kv = pl.program_id(2)

    @pl.when(kv == 0)
    def _():
        m_ref[...] = jnp.full_like(m_ref, -jnp.inf)
        l_ref[...] = jnp.zeros_like(l_ref)
        acc_ref[...] = jnp.zeros_like(acc_ref)

    k = k_ref[...]
    v = v_ref[...]
    for hh in range(rep):
        sl = slice(hh * ATT_HEAD, (hh + 1) * ATT_HEAD)
        s = lax.dot_general(q_ref[:, sl], k, (((1,), (1,)), ((), ())), preferred_element_type=F32)
        m_prev = m_ref[hh]
        m_new = jnp.maximum(m_prev, jnp.max(s, axis=-1, keepdims=True))
        alpha = jnp.exp(m_prev - m_new)
        p = jnp.exp(s - m_new[:, :1])
        l_ref[hh] = alpha * l_ref[hh] + jnp.sum(p, axis=-1, keepdims=True)
        m_ref[hh] = m_new
        acc_ref[:, sl] = alpha * acc_ref[:, sl] + jnp.dot(p.astype(BF16), v, preferred_element_type=F32)

    @pl.when(kv == pl.num_programs(2) - 1)
    def _():
        for hh in range(rep):
            sl = slice(hh * ATT_HEAD, (hh + 1) * ATT_HEAD)
            o_ref[:, sl] = (acc_ref[:, sl] / l_ref[hh]).astype(o_ref.dtype)


def flash_call(q, k, v, kv_heads, *, tq=512, tk=512, interpret=False):
    T, qw = q.shape
    rep = qw // ATT_HEAD // kv_heads
    tq, tk = min(tq, T), min(tk, T)
    gw = rep * ATT_HEAD
    return pl.pallas_call(
        functools.partial(_flash_kernel, rep=rep),
        grid=(kv_heads, T // tq, T // tk),
        in_specs=[pl.BlockSpec((tq, gw), lambda g, i, j: (i, g)),
                  pl.BlockSpec((tk, ATT_HEAD), lambda g, i, j: (j, g)),
                  pl.BlockSpec((tk, ATT_HEAD), lambda g, i, j: (j, g))],
        out_specs=pl.BlockSpec((tq, gw), lambda g, i, j: (i, g)),
        out_shape=jax.ShapeDtypeStruct((T, qw), BF16),
        scratch_shapes=[pltpu.VMEM((rep, tq, ATT_HEAD), F32), pltpu.VMEM((rep, tq, ATT_HEAD), F32),
                        pltpu.VMEM((tq, gw), F32)],
        compiler_params=_params(("arbitrary", "arbitrary", "arbitrary"), 48),
        name="gqa_flash",
        interpret=interpret,
    )(q, k, v)


def _router_kernel(h_ref, g_ref, wr_ref, u_ref, gate_ref, *, n_experts):
    u = _rms(h_ref[...], g_ref[...])
    u_ref[...] = u.astype(u_ref.dtype)
    logits = _dot_f32(u, wr_ref[...])
    lane = lax.broadcasted_iota(jnp.int32, logits.shape, 1)
    logits = jnp.where(lane < n_experts, logits, -jnp.inf)
    m1 = jnp.max(logits, axis=-1, keepdims=True)
    i1 = jnp.min(jnp.where(logits == m1, lane, LANES), axis=-1, keepdims=True)
    rest = jnp.where(lane == i1, -jnp.inf, logits)
    m2 = jnp.max(rest, axis=-1, keepdims=True)
    i2 = jnp.min(jnp.where(rest == m2, lane, LANES), axis=-1, keepdims=True)
    e2 = jnp.exp(m2 - m1)
    p1 = 1.0 / (1.0 + e2)
    p2 = e2 / (1.0 + e2)
    gate_ref[...] = jnp.where(lane == i1, p1, 0.0) + jnp.where(lane == i2, p2, 0.0)


def router_call(h, g, w_router, *, tm=256, interpret=False):
    T, D = h.shape
    tm = min(tm, T)
    n_experts = w_router.shape[1]
    wr = _pad_to(w_router, 1, LANES)
    return pl.pallas_call(
        functools.partial(_router_kernel, n_experts=n_experts),
        grid=(T // tm,),
        in_specs=[pl.BlockSpec((tm, D), lambda i: (i, 0)), pl.BlockSpec((1, D), lambda i: (0, 0)),
                  pl.BlockSpec((D, LANES), lambda i: (0, 0))],
        out_specs=[pl.BlockSpec((tm, D), lambda i: (i, 0)), pl.BlockSpec((tm, LANES), lambda i: (i, 0))],
        out_shape=[jax.ShapeDtypeStruct((T, D), BF16), jax.ShapeDtypeStruct((T, LANES), F32)],
        compiler_params=_params(("arbitrary",), 32),
        name="moe_router",
        interpret=interpret,
    )(h, g.reshape(1, D), wr)


def rwkv_layer(h, norm_g, mix, w_rkv, w0, w1, w2, a0, a1, a2, g1, g2, k_k, k_a, r_k, ln_w, ln_b, w_out,
               interpret=False):
    T, D = h.shape
    order = jnp.array([0, 2, 3, 1, 4, 5])
    xs = mixprep_call(h, norm_g, mix[order], interpret=interpret)
    rkv = matmul_call(xs, w_rkv, name="rwkv_rkv", interpret=interpret)
    g, logw, a = lora_call(xs, w0, w1, w2, a0, a1, a2, g1, g2, interpret=interpret)
    y, bonus = wkv7_bidir(rkv[0], rkv[1], rkv[2], logw, a, k_k, k_a, r_k.reshape(D), interpret=interpret)
    z = wkv_post_call(y, bonus, g, ln_w, ln_b, interpret=interpret)
    return matmul_call(z[None], w_out[None], h[None], name="rwkv_out", interpret=interpret)[0]


def attn_layer(h, norm_g, w_qkv, q_norm, k_norm, w_o, grid_w, interpret=False):
    T, D = h.shape
    q_heads = w_o.shape[0] // ATT_HEAD
    kv_heads = (w_qkv.shape[1] // ATT_HEAD - q_heads) // 2
    u = rms_norm_call(h, norm_g, BF16, interpret=interpret)
    qkv = matmul_call(u[None], w_qkv[None], name="attn_qkv", interpret=interpret)[0]
    cos, sin = _rope_tables(T, grid_w)
    q, k, v = qkprep_call(qkv, q_norm, k_norm, cos, sin, q_heads, kv_heads, interpret=interpret)
    o = flash_call(q, k, v, kv_heads, interpret=interpret)
    return matmul_call(o[None], w_o[None], h[None], name="attn_out", interpret=interpret)[0]


def ffn_layer(h, norm_g, wg, wu, wd, interpret=False):
    u = rms_norm_call(h, norm_g, BF16, interpret=interpret)
    return swiglu_call(u, wg[None], wu[None], wd[None], h, interpret=interpret)


def moe_layer(h, norm_g, w_router, wg, wu, wd, interpret=False):
    u, gates = router_call(h, norm_g, w_router, interpret=interpret)
    return swiglu_call(u, wg, wu, wd, h, gates, interpret=interpret)


GRID_W = 64


def kernel(x, norm_mix_g, norm_ffn_g, rwkv_mix, rwkv_w_rkv, rwkv_w0, rwkv_w1, rwkv_w2, rwkv_a0, rwkv_a1, rwkv_a2, rwkv_g1, rwkv_g2, rwkv_k_k, rwkv_k_a, rwkv_r_k, rwkv_ln_w, rwkv_ln_b, rwkv_w_out, attn_w_qkv, attn_q_norm, attn_k_norm, attn_w_o, ffn_w_gate, ffn_w_up, ffn_w_down, moe_router, moe_w_gate, moe_w_up, moe_w_down, final_norm_g):
    B, T, D = x.shape
    depth = norm_mix_g.shape[0]
    outs = []
    for bi in range(B):
        h = x[bi]
        for i in range(depth):
            j = i // 2
            if i % 2 == 0:
                h = rwkv_layer(h, norm_mix_g[i], rwkv_mix[j], rwkv_w_rkv[j], rwkv_w0[j], rwkv_w1[j], rwkv_w2[j],
                               rwkv_a0[j], rwkv_a1[j], rwkv_a2[j], rwkv_g1[j], rwkv_g2[j], rwkv_k_k[j],
                               rwkv_k_a[j], rwkv_r_k[j], rwkv_ln_w[j], rwkv_ln_b[j], rwkv_w_out[j])
                h = ffn_layer(h, norm_ffn_g[i], ffn_w_gate[j], ffn_w_up[j], ffn_w_down[j])
            else:
                h = attn_layer(h, norm_mix_g[i], attn_w_qkv[j], attn_q_norm[j], attn_k_norm[j], attn_w_o[j], GRID_W)
                h = moe_layer(h, norm_ffn_g[i], moe_router[j], moe_w_gate[j], moe_w_up[j], moe_w_down[j])
        outs.append(rms_norm_call(h, final_norm_g, x.dtype))
    return jnp.stack(outs)
```

```python
import functools
import math

import jax
import jax.numpy as jnp
from jax import lax
from jax.experimental import pallas as pl
from jax.experimental.pallas import tpu as pltpu

F32 = jnp.float32
BF16 = jnp.bfloat16
HIGHEST = lax.Precision.HIGHEST

LANES = 128
RWKV_HEAD = 64
WKV_CHUNK = 64
GN_EPS = 64e-5
NORM_EPS = 1e-6


def _dot(a, b):
    return jnp.dot(a.astype(BF16), b.astype(BF16), preferred_element_type=F32)


def _dot_nt(a, b):
    return lax.dot_general(a.astype(BF16), b.astype(BF16), (((1,), (1,)), ((), ())),
                           preferred_element_type=F32)


def _dot_tn(a, b):
    return lax.dot_general(a.astype(BF16), b.astype(BF16), (((0,), (0,)), ((), ())),
                           preferred_element_type=F32)


def _dot_f32(a, b):
    return jnp.dot(a, b, preferred_element_type=F32, precision=HIGHEST)


def _dot_split(x, m, pieces, exact_lhs=False):
    mb = m.astype(BF16)
    acc = None
    rem = x
    for _ in range(pieces):
        part = rem.astype(BF16)
        rem = rem - part.astype(F32)
        t = (jnp.dot(mb, part, preferred_element_type=F32) if exact_lhs
             else jnp.dot(part, mb, preferred_element_type=F32))
        acc = t if acc is None else acc + t
    return acc


def _wkv_kernel(r_ref, k_ref, v_ref, lw_ref, a_ref, kk_ref, ka_ref, rk_ref,
                y_ref, bonus_ref, s_ref, *, pairs):
    L = WKV_CHUNK
    d = pl.program_id(0)
    c = pl.program_id(2)

    @pl.when(c == 0)
    def _():
        s_ref[...] = jnp.zeros_like(s_ref)

    sign = 1 - 2 * d
    row = lax.broadcasted_iota(jnp.int32, (L, 2 * L), 0)
    col = lax.broadcasted_iota(jnp.int32, (L, 2 * L), 1)
    order = (row - (col & (L - 1))) * sign
    strict = order > 0
    incl = order >= 0
    first_head_cols = col < L
    r2 = lax.broadcasted_iota(jnp.int32, (L, L), 0)
    c2 = lax.broadcasted_iota(jnp.int32, (L, L), 1)
    tri = jnp.where((r2 - c2) * sign >= 0, 1.0, 0.0).astype(F32)
    lane = lax.broadcasted_iota(jnp.int32, (L, LANES), 1)
    first_head = lane < RWKV_HEAD
    hr = lax.broadcasted_iota(jnp.int32, (LANES, LANES), 0)
    hc = lax.broadcasted_iota(jnp.int32, (LANES, LANES), 1)
    head_ones = jnp.where((hr < RWKV_HEAD) == (hc < RWKV_HEAD), 1.0, 0.0).astype(F32)

    def stack(x):
        return jnp.concatenate([jnp.where(first_head, x, 0.0), jnp.where(first_head, 0.0, x)], axis=0)

    def block_diag(m):
        return jnp.concatenate([jnp.where(first_head_cols, m, 0.0), jnp.where(first_head_cols, 0.0, m)], axis=0)

    P = range(pairs)
    sls = [slice(p * LANES, (p + 1) * LANES) for p in P]
    each = lambda fn: [fn(p) for p in P]
    r = each(lambda p: r_ref[:, sls[p]])
    k = each(lambda p: k_ref[:, sls[p]])
    v = each(lambda p: v_ref[:, sls[p]])
    lw = each(lambda p: lw_ref[0, :, sls[p]])
    a = each(lambda p: a_ref[0, :, sls[p]])

    kkr = each(lambda p: k[p] * kk_ref[:, sls[p]])
    ss = each(lambda p: _dot_split(kkr[p] * kkr[p], head_ones, 2))
    cs = each(lambda p: _dot_split(lw[p], tri, 3, exact_lhs=True))
    kd = each(lambda p: k[p] * (1.0 + (a[p] - 1.0) * ka_ref[:, sls[p]]))
    bsum = each(lambda p: _dot_split(r[p] * kd[p] * rk_ref[:, sls[p]], head_ones, 2))
    for p in P:
        bonus_ref[0, :, sls[p]] = bsum[p] * v[p]
    kk = each(lambda p: kkr[p] / jnp.maximum(jnp.sqrt(ss[p]), 1e-12))
    b = each(lambda p: kk[p] * a[p])
    tot = each(lambda p: jnp.sum(lw[p], axis=0, keepdims=True))
    a_t = each(lambda p: -kk[p] * jnp.exp(cs[p] - lw[p]))
    r_t = each(lambda p: r[p] * jnp.exp(cs[p]))
    g_inv = each(lambda p: jnp.exp(-cs[p]))
    g_rem = each(lambda p: jnp.exp(tot[p] - cs[p]))

    sc = each(lambda p: _dot_nt(jnp.concatenate([a_t[p], r_t[p]], axis=0),
                                jnp.concatenate([stack(b[p] * g_inv[p]), stack(kd[p] * g_inv[p])], axis=0)))
    m_ak = each(lambda p: jnp.where(strict, sc[p][:L, 2 * L:], 0.0))
    m_rb = each(lambda p: jnp.where(incl, sc[p][L:, :2 * L], 0.0))
    m_rk = each(lambda p: jnp.where(incl, sc[p][L:, 2 * L:], 0.0))

    pw = each(lambda p: block_diag(jnp.where(strict, sc[p][:L, :2 * L], 0.0)))
    n = pw
    for _ in range(int(math.log2(L)) - 1):
        pw = each(lambda p: _dot(pw[p], pw[p]))
        n = each(lambda p: n[p] + pw[p] + _dot(n[p], pw[p]))

    s = each(lambda p: s_ref[p])
    vs = each(lambda p: stack(v[p]))
    x = each(lambda p: _dot_nt(a_t[p], s[p]) + _dot(m_ak[p], vs[p]))
    xs = each(lambda p: stack(x[p]))
    us = each(lambda p: xs[p] + _dot(n[p], xs[p]))
    for p in P:
        y_ref[0, :, sls[p]] = _dot_nt(r_t[p], s[p]) + _dot(m_rb[p], us[p]) + _dot(m_rk[p], vs[p])
    for p in P:
        s_ref[p] = s[p] * jnp.exp(tot[p]) + _dot_tn(
            jnp.concatenate([us[p], vs[p]], axis=0),
            jnp.concatenate([stack(b[p] * g_rem[p]), stack(kd[p] * g_rem[p])], axis=0))


def wkv7_bidir(r, k, v, logw, a, k_k, k_a, r_k, *, pairs=8, interpret=False):
    T, D = r.shape
    L = WKV_CHUNK
    nc = T // L
    pairs = min(pairs, D // LANES)
    W = LANES * pairs
    nj = D // W
    row_map = lambda d, j, c: (c + d * (nc - 1 - 2 * c), j)
    dir_map = lambda d, j, c: (d, c + d * (nc - 1 - 2 * c), j)
    par_map = lambda d, j, c: (0, j)
    tok = pl.BlockSpec((L, W), row_map)
    dtok = pl.BlockSpec((1, L, W), dir_map)
    par = pl.BlockSpec((1, W), par_map)
    return pl.pallas_call(
        functools.partial(_wkv_kernel, pairs=pairs),
        grid=(2, nj, nc),
        in_specs=[tok, tok, tok, dtok, dtok, par, par, par],
        out_specs=[dtok, dtok],
        out_shape=[jax.ShapeDtypeStruct((2, T, D), F32)] * 2,
        scratch_shapes=[pltpu.VMEM((pairs, LANES, LANES), F32)],
        compiler_params=pltpu.CompilerParams(dimension_semantics=("arbitrary", "arbitrary", "arbitrary")),
        name="wkv7_bidir",
        interpret=interpret,
    )(r, k, v, logw, a, k_k.reshape(1, D), k_a.reshape(1, D), r_k.reshape(1, D))


def _params(sem, vmem_mb):
    return pltpu.CompilerParams(dimension_semantics=sem, vmem_limit_bytes=vmem_mb * 1024 * 1024)


def _rms(x, g):
    return x * lax.rsqrt(jnp.mean(x * x, axis=-1, keepdims=True) + NORM_EPS) * g


def _norm_kernel(h_ref, g_ref, o_ref):
    o_ref[...] = _rms(h_ref[...], g_ref[...]).astype(o_ref.dtype)


def rms_norm_call(h, g, out_dtype, *, tm=256, interpret=False):
    T, D = h.shape
    tm = min(tm, T)
    return pl.pallas_call(
        _norm_kernel,
        grid=(T // tm,),
        in_specs=[pl.BlockSpec((tm, D), lambda i: (i, 0)), pl.BlockSpec((1, D), lambda i: (0, 0))],
        out_specs=pl.BlockSpec((tm, D), lambda i: (i, 0)),
        out_shape=jax.ShapeDtypeStruct((T, D), out_dtype),
        compiler_params=_params(("arbitrary",), 32),
        name="rms_norm",
        interpret=interpret,
    )(h, g.reshape(1, D))


HALO = 8


def _mixprep_kernel(h_ref, hp_ref, hn_ref, g_ref, mix_ref, o_ref):
    i = pl.program_id(0)
    last = pl.num_programs(0) - 1
    g = g_ref[...]
    u = _rms(h_ref[...], g)
    tm = u.shape[0]
    up = jnp.where(i == 0, 0.0, _rms(hp_ref[HALO - 1:HALO, :], g))
    un = jnp.where(i == last, 0.0, _rms(hn_ref[0:1, :], g))
    rows = lax.broadcasted_iota(jnp.int32, u.shape, 0)
    prev = jnp.where(rows == 0, up, pltpu.roll(u, 1, 0))
    nxt = jnp.where(rows == tm - 1, un, pltpu.roll(u, tm - 1, 0))
    xx = 0.5 * (prev + nxt) - u
    for m in range(o_ref.shape[0]):
        o_ref[m] = (u + xx * mix_ref[m:m + 1, :]).astype(o_ref.dtype)


def mixprep_call(h, g, mix, *, tm=256, interpret=False):
    T, D = h.shape
    tm = min(tm, T)
    nm = mix.shape[0]
    nb = tm // HALO
    nh = T // HALO
    return pl.pallas_call(
        _mixprep_kernel,
        grid=(T // tm,),
        in_specs=[pl.BlockSpec((tm, D), lambda i: (i, 0)),
                  pl.BlockSpec((HALO, D), lambda i: (jnp.maximum(i * nb - 1, 0), 0)),
                  pl.BlockSpec((HALO, D), lambda i: (jnp.minimum((i + 1) * nb, nh - 1), 0)),
                  pl.BlockSpec((1, D), lambda i: (0, 0)),
                  pl.BlockSpec((nm, D), lambda i: (0, 0))],
        out_specs=pl.BlockSpec((nm, tm, D), lambda i: (0, i, 0)),
        out_shape=jax.ShapeDtypeStruct((nm, T, D), BF16),
        compiler_params=_params(("arbitrary",), 48),
        name="rwkv_mixprep",
        interpret=interpret,
    )(h, h, h, g.reshape(1, D), mix)


def _mm_kernel(*refs, has_resid):
    if has_resid:
        x_ref, w_ref, r_ref, o_ref, wb_ref = refs
    else:
        x_ref, w_ref, o_ref, wb_ref = refs

    @pl.when(pl.program_id(2) == 0)
    def _():
        wb_ref[...] = w_ref[...].astype(BF16)

    acc = jnp.dot(x_ref[...], wb_ref[...], preferred_element_type=F32)
    if has_resid:
        acc = acc + r_ref[...]
    o_ref[...] = acc.astype(o_ref.dtype)


def matmul_call(x, w, resid=None, *, out_dtype=F32, tm=512, tn=512, name="matmul", interpret=False):
    _, T, Kd = x.shape
    S, _, N = w.shape
    tm, tn = min(tm, T), min(tn, N)
    in_specs = [pl.BlockSpec((None, tm, Kd), lambda s, j, i: (s, i, 0)),
                pl.BlockSpec((None, Kd, tn), lambda s, j, i: (s, 0, j))]
    args = [x, w]
    if resid is not None:
        in_specs.append(pl.BlockSpec((None, tm, tn), lambda s, j, i: (s, i, j)))
        args.append(resid)
    return pl.pallas_call(
        functools.partial(_mm_kernel, has_resid=resid is not None),
        grid=(S, N // tn, T // tm),
        in_specs=in_specs,
        out_specs=pl.BlockSpec((None, tm, tn), lambda s, j, i: (s, i, j)),
        out_shape=jax.ShapeDtypeStruct((S, T, N), out_dtype),
        scratch_shapes=[pltpu.VMEM((Kd, tn), BF16)],
        compiler_params=_params(("arbitrary", "arbitrary", "arbitrary"), 48),
        name=name,
        interpret=interpret,
    )(*args)


def _lora_kernel(xw_ref, xa_ref, xg_ref, w1_ref, w2_ref, w0_ref, a1_ref, a2_ref, a0_ref, g1_ref, g2_ref,
                 g_ref, lw_ref, a_ref):
    xw = xw_ref[...]
    xa = xa_ref[...]
    g_ref[...] = _dot(jax.nn.sigmoid(_dot(xg_ref[...], g1_ref[...])), g2_ref[...])
    for d in range(2):
        wl = w0_ref[d:d + 1, :] + _dot(jnp.tanh(_dot(xw, w1_ref[d])), w2_ref[d])
        lw_ref[d] = -jnp.exp(-jax.nn.softplus(-wl) - 0.5)
        a_ref[d] = jax.nn.sigmoid(a0_ref[d:d + 1, :] + _dot(_dot(xa, a1_ref[d]), a2_ref[d]))


def _pad_to(x, axis, mult):
    pad = (-x.shape[axis]) % mult
    if pad == 0:
        return x
    widths = [(0, 0)] * x.ndim
    widths[axis] = (0, pad)
    return jnp.pad(x, widths)


def lora_call(xs, w0, w1, w2, a0, a1, a2, g1, g2, *, tm=256, interpret=False):
    _, T, D = xs.shape
    tm = min(tm, T)
    w1 = _pad_to(w1, 2, LANES).astype(BF16)
    w2 = _pad_to(w2, 1, LANES).astype(BF16)
    a1 = _pad_to(a1, 2, LANES).astype(BF16)
    a2 = _pad_to(a2, 1, LANES).astype(BF16)
    g1 = g1.astype(BF16)
    g2 = g2.astype(BF16)
    xspec = lambda m: pl.BlockSpec((None, tm, D), lambda i: (m, i, 0))
    full = lambda a: pl.BlockSpec(a.shape, lambda i: (0,) * a.ndim)
    return pl.pallas_call(
        _lora_kernel,
        grid=(T // tm,),
        in_specs=[xspec(3), xspec(4), xspec(5), full(w1), full(w2), full(w0), full(a1), full(a2), full(a0),
                  full(g1), full(g2)],
        out_specs=[pl.BlockSpec((tm, D), lambda i: (i, 0)),
                   pl.BlockSpec((2, tm, D), lambda i: (0, i, 0)),
                   pl.BlockSpec((2, tm, D), lambda i: (0, i, 0))],
        out_shape=[jax.ShapeDtypeStruct((T, D), F32), jax.ShapeDtypeStruct((2, T, D), F32),
                   jax.ShapeDtypeStruct((2, T, D), F32)],
        compiler_params=_params(("arbitrary",), 56),
        name="rwkv_lora",
        interpret=interpret,
    )(xs, xs, xs, w1, w2, w0, a1, a2, a0, g1, g2)


def _wkv_post_kernel(y_ref, b_ref, g_ref, lnw_ref, lnb_ref, o_ref):
    hr = lax.broadcasted_iota(jnp.int32, (LANES, LANES), 0)
    hc = lax.broadcasted_iota(jnp.int32, (LANES, LANES), 1)
    head_mean = jnp.where((hr < RWKV_HEAD) == (hc < RWKV_HEAD), 1.0 / RWKV_HEAD, 0.0).astype(F32)
    for p in range(o_ref.shape[1] // LANES):
        sl = slice(p * LANES, (p + 1) * LANES)
        y = y_ref[0, :, sl] + y_ref[1, :, sl]
        yc = y - _dot_f32(y, head_mean)
        var = _dot_f32(yc * yc, head_mean)
        yn = yc * lax.rsqrt(var + GN_EPS)
        out = yn * lnw_ref[:, sl] + lnb_ref[:, sl] + b_ref[0, :, sl] + b_ref[1, :, sl]
        o_ref[:, sl] = (out * g_ref[:, sl]).astype(o_ref.dtype)


def wkv_post_call(y, bonus, g, ln_w, ln_b, *, tm=256, interpret=False):
    _, T, D = y.shape
    tm = min(tm, T)
    two = pl.BlockSpec((2, tm, D), lambda i: (0, i, 0))
    one = pl.BlockSpec((tm, D), lambda i: (i, 0))
    par = pl.BlockSpec((1, D), lambda i: (0, 0))
    return pl.pallas_call(
        _wkv_post_kernel,
        grid=(T // tm,),
        in_specs=[two, two, one, par, par],
        out_specs=one,
        out_shape=jax.ShapeDtypeStruct((T, D), BF16),
        compiler_params=_params(("arbitrary",), 48),
        name="wkv_post",
        interpret=interpret,
    )(y, bonus, g, ln_w.reshape(1, D), ln_b.reshape(1, D))


def _swiglu_kernel(x_ref, wg_ref, wu_ref, wd_ref, r_ref, o_ref):
    @pl.when(pl.program_id(1) == 0)
    def _():
        o_ref[...] = r_ref[...]

    x = x_ref[...]
    hg = jnp.dot(x, wg_ref[...].astype(BF16), preferred_element_type=F32)
    hu = jnp.dot(x, wu_ref[...].astype(BF16), preferred_element_type=F32)
    h = hg * jax.nn.sigmoid(hg) * hu
    o_ref[...] += jnp.dot(h.astype(BF16), wd_ref[...].astype(BF16), preferred_element_type=F32)


def swiglu_call(x, wg, wu, wd, resid, *, tm=512, tf=512, interpret=False):
    T, D = x.shape
    Fd = wg.shape[1]
    tm, tf = min(tm, T), min(tf, Fd)
    return pl.pallas_call(
        _swiglu_kernel,
        grid=(T // tm, Fd // tf),
        in_specs=[pl.BlockSpec((tm, D), lambda i, f: (i, 0)),
                  pl.BlockSpec((D, tf), lambda i, f: (0, f)),
                  pl.BlockSpec((D, tf), lambda i, f: (0, f)),
                  pl.BlockSpec((tf, D), lambda i, f: (f, 0)),
                  pl.BlockSpec((tm, D), lambda i, f: (i, 0))],
        out_specs=pl.BlockSpec((tm, D), lambda i, f: (i, 0)),
        out_shape=jax.ShapeDtypeStruct((T, D), F32),
        compiler_params=_params(("arbitrary", "arbitrary"), 56),
        name="swiglu",
        interpret=interpret,
    )(x, wg, wu, wd, resid)


ATT_HEAD = 128
ROPE_HALF = 32


def _qkprep_kernel(qkv_ref, qn_ref, kn_ref, cos_ref, sin_ref, q_ref, k_ref, v_ref, *, q_heads, kv_heads):
    cos = cos_ref[...]
    sin = sin_ref[...]
    lane = lax.broadcasted_iota(jnp.int32, cos.shape, 1)
    low = (lane & (2 * ROPE_HALF - 1)) < ROPE_HALF
    scale = math.log2(math.e) / math.sqrt(ATT_HEAD)

    def head(x, g):
        xn = _rms(x, g)
        partner = jnp.where(low, pltpu.roll(xn, ATT_HEAD - ROPE_HALF, 1), pltpu.roll(xn, ROPE_HALF, 1))
        return xn * cos + partner * sin

    for hh in range(q_heads):
        sl = slice(hh * ATT_HEAD, (hh + 1) * ATT_HEAD)
        q_ref[:, sl] = (head(qkv_ref[:, sl], qn_ref[...]) * scale).astype(q_ref.dtype)
    for hh in range(kv_heads):
        src = slice((q_heads + hh) * ATT_HEAD, (q_heads + hh + 1) * ATT_HEAD)
        dst = slice(hh * ATT_HEAD, (hh + 1) * ATT_HEAD)
        k_ref[:, dst] = head(qkv_ref[:, src], kn_ref[...]).astype(k_ref.dtype)
    v0 = (q_heads + kv_heads) * ATT_HEAD
    v_ref[...] = qkv_ref[:, v0:].astype(v_ref.dtype)


def qkprep_call(qkv, q_norm, k_norm, cos, sin, q_heads, kv_heads, *, tm=256, interpret=False):
    T, Wd = qkv.shape
    tm = min(tm, T)
    qw, kw = q_heads * ATT_HEAD, kv_heads * ATT_HEAD
    row = lambda w: pl.BlockSpec((tm, w), lambda i: (i, 0))
    par = pl.BlockSpec((1, ATT_HEAD), lambda i: (0, 0))
    return pl.pallas_call(
        functools.partial(_qkprep_kernel, q_heads=q_heads, kv_heads=kv_heads),
        grid=(T // tm,),
        in_specs=[row(Wd), par, par, row(ATT_HEAD), row(ATT_HEAD)],
        out_specs=[row(qw), row(kw), row(kw)],
        out_shape=[jax.ShapeDtypeStruct((T, qw), BF16), jax.ShapeDtypeStruct((T, kw), BF16),
                   jax.ShapeDtypeStruct((T, kw), BF16)],
        compiler_params=_params(("arbitrary",), 32),
        name="attn_qkprep",
        interpret=interpret,
    )(qkv, q_norm.reshape(1, ATT_HEAD), k_norm.reshape(1, ATT_HEAD), cos, sin)


def _rope_tables(T, grid_w):
    pos = jnp.arange(T)
    row_ids = (pos // grid_w).astype(F32)
    col_ids = (pos % grid_w).astype(F32)
    axis_dim = ATT_HEAD // 2
    inv = 10000.0 ** (-jnp.arange(0, axis_dim, 2, dtype=F32) / axis_dim)
    ang_r = row_ids[:, None] * inv[None, :]
    ang_c = col_ids[:, None] * inv[None, :]
    cos = jnp.concatenate([jnp.cos(ang_r), jnp.cos(ang_r), jnp.cos(ang_c), jnp.cos(ang_c)], axis=-1)
    sin = jnp.concatenate([-jnp.sin(ang_r), jnp.sin(ang_r), -jnp.sin(ang_c), jnp.sin(ang_c)], axis=-1)
    return cos, sin


def _flash_kernel(q_ref, k_ref, v_ref, o_ref, m_ref, l_ref, acc_ref, *, rep):
    kv = pl.program_id(2)

    @pl.when(kv == 0)
    def _():
        m_ref[...] = jnp.full_like(m_ref, -jnp.inf)
        l_ref[...] = jnp.zeros_like(l_ref)
        acc_ref[...] = jnp.zeros_like(acc_ref)

    k = k_ref[...]
    v = v_ref[...]
    for hh in range(rep):
        sl = slice(hh * ATT_HEAD, (hh + 1) * ATT_HEAD)
        s = lax.dot_general(q_ref[:, sl], k, (((1,), (1,)), ((), ())), preferred_element_type=F32)
        m_prev = m_ref[hh]
        m_new = jnp.maximum(m_prev, jnp.max(s, axis=-1, keepdims=True))
        alpha = jnp.exp2(m_prev - m_new)
        p = jnp.exp2(s - m_new[:, :1])
        l_ref[hh] = alpha * l_ref[hh] + jnp.sum(p, axis=-1, keepdims=True)
        m_ref[hh] = m_new
        acc_ref[:, sl] = alpha * acc_ref[:, sl] + jnp.dot(p.astype(BF16), v, preferred_element_type=F32)

    @pl.when(kv == pl.num_programs(2) - 1)
    def _():
        for hh in range(rep):
            sl = slice(hh * ATT_HEAD, (hh + 1) * ATT_HEAD)
            o_ref[:, sl] = (acc_ref[:, sl] / l_ref[hh]).astype(o_ref.dtype)


def flash_call(q, k, v, kv_heads, *, tq=512, tk=512, interpret=False):
    T, qw = q.shape
    rep = qw // ATT_HEAD // kv_heads
    tq, tk = min(tq, T), min(tk, T)
    gw = rep * ATT_HEAD
    return pl.pallas_call(
        functools.partial(_flash_kernel, rep=rep),
        grid=(kv_heads, T // tq, T // tk),
        in_specs=[pl.BlockSpec((tq, gw), lambda g, i, j: (i, g)),
                  pl.BlockSpec((tk, ATT_HEAD), lambda g, i, j: (j, g)),
                  pl.BlockSpec((tk, ATT_HEAD), lambda g, i, j: (j, g))],
        out_specs=pl.BlockSpec((tq, gw), lambda g, i, j: (i, g)),
        out_shape=jax.ShapeDtypeStruct((T, qw), BF16),
        scratch_shapes=[pltpu.VMEM((rep, tq, ATT_HEAD), F32), pltpu.VMEM((rep, tq, ATT_HEAD), F32),
                        pltpu.VMEM((tq, gw), F32)],
        compiler_params=_params(("arbitrary", "arbitrary", "arbitrary"), 48),
        name="gqa_flash",
        interpret=interpret,
    )(q, k, v)


(INFO_E1, INFO_E2, INFO_RANK1, INFO_RANK2, INFO_P1, INFO_P2) = range(6)


def _router_kernel(h_ref, g_ref, wr_ref, u_ref, info_ref, cnt_ref, *, n_experts):
    @pl.when(pl.program_id(0) == 0)
    def _():
        cnt_ref[...] = jnp.zeros_like(cnt_ref)

    u = _rms(h_ref[...], g_ref[...])
    u_ref[...] = u
    logits = _dot_f32(u, wr_ref[...])
    lane = lax.broadcasted_iota(jnp.int32, logits.shape, 1)
    logits = jnp.where(lane < n_experts, logits, -jnp.inf)
    m1 = jnp.max(logits, axis=-1, keepdims=True)
    i1 = jnp.min(jnp.where(logits == m1, lane, LANES), axis=-1, keepdims=True)
    rest = jnp.where(lane == i1, -jnp.inf, logits)
    m2 = jnp.max(rest, axis=-1, keepdims=True)
    i2 = jnp.min(jnp.where(rest == m2, lane, LANES), axis=-1, keepdims=True)
    e2 = jnp.exp(m2 - m1)
    p1 = 1.0 / (1.0 + e2)
    p2 = e2 / (1.0 + e2)

    oh1 = jnp.where(lane == i1, 1.0, 0.0)
    oh2 = jnp.where(lane == i2, 1.0, 0.0)
    both = oh1 + oh2
    tm = both.shape[0]
    tr = lax.broadcasted_iota(jnp.int32, (tm, tm), 0)
    tc = lax.broadcasted_iota(jnp.int32, (tm, tm), 1)
    cum = _dot(jnp.where(tr >= tc, 1.0, 0.0), both)
    before = cum - both + cnt_ref[...]
    rank1 = jnp.sum(oh1 * before, axis=-1, keepdims=True)
    rank2 = jnp.sum(oh2 * before, axis=-1, keepdims=True)
    cnt_ref[...] += cum[tm - 1:tm, :]

    info = jnp.zeros_like(logits)
    for idx, val in ((INFO_E1, i1.astype(F32)), (INFO_E2, i2.astype(F32)), (INFO_RANK1, rank1),
                     (INFO_RANK2, rank2), (INFO_P1, p1), (INFO_P2, p2)):
        info = jnp.where(lane == idx, val, info)
    info_ref[...] = info


def router_call(h, g, w_router, *, tm=256, interpret=False):
    T, D = h.shape
    tm = min(tm, T)
    n_experts = w_router.shape[1]
    wr = _pad_to(w_router, 1, LANES)
    return pl.pallas_call(
        functools.partial(_router_kernel, n_experts=n_experts),
        grid=(T // tm,),
        in_specs=[pl.BlockSpec((tm, D), lambda i: (i, 0)), pl.BlockSpec((1, D), lambda i: (0, 0)),
                  pl.BlockSpec((D, LANES), lambda i: (0, 0))],
        out_specs=[pl.BlockSpec((tm, D), lambda i: (i, 0)), pl.BlockSpec((tm, LANES), lambda i: (i, 0)),
                   pl.BlockSpec((1, LANES), lambda i: (0, 0))],
        out_shape=[jax.ShapeDtypeStruct((T, D), F32), jax.ShapeDtypeStruct((T, LANES), F32),
                   jax.ShapeDtypeStruct((1, LANES), F32)],
        compiler_params=_params(("arbitrary",), 32),
        name="moe_router",
        interpret=interpret,
    )(h, g.reshape(1, D), wr)


def _dispatch_kernel(pos_ref, u_hbm, xs_init_hbm, xs_hbm, sem, *, tm, n_tok):
    del xs_init_hbm
    base = pl.program_id(0) * tm

    def row_copies(t):
        src = u_hbm.at[pl.ds(t, 1)]
        return (pltpu.make_async_copy(src, xs_hbm.at[pl.ds(pos_ref[t], 1)], sem),
                pltpu.make_async_copy(src, xs_hbm.at[pl.ds(pos_ref[n_tok + t], 1)], sem))

    def issue(r, carry):
        for cp in row_copies(base + r):
            cp.start()
        return carry

    def drain(r, carry):
        for cp in row_copies(base + r):
            cp.wait()
        return carry

    lax.fori_loop(0, tm, issue, 0)
    lax.fori_loop(0, tm, drain, 0)


def dispatch_call(u, pos, n_rows, *, tm=256, interpret=False):
    T, D = u.shape
    tm = min(tm, T)
    xs0 = jnp.zeros((n_rows, D), u.dtype)
    return pl.pallas_call(
        functools.partial(_dispatch_kernel, tm=tm, n_tok=T),
        grid_spec=pltpu.PrefetchScalarGridSpec(
            num_scalar_prefetch=1,
            grid=(T // tm,),
            in_specs=[pl.BlockSpec(memory_space=pl.ANY), pl.BlockSpec(memory_space=pl.ANY)],
            out_specs=pl.BlockSpec(memory_space=pl.ANY),
            scratch_shapes=[pltpu.SemaphoreType.DMA(())]),
        out_shape=jax.ShapeDtypeStruct((n_rows, D), u.dtype),
        input_output_aliases={2: 0},
        compiler_params=_params(("arbitrary",), 32),
        name="moe_dispatch",
        interpret=interpret,
    )(pos, u, xs0)


def _experts_kernel(te_ref, nused_ref, x_ref, wg_ref, wu_ref, wd_ref, o_ref, xb_ref):
    del te_ref
    i = pl.program_id(0)
    f = pl.program_id(1)

    @pl.when(i < nused_ref[0])
    def _():
        @pl.when(f == 0)
        def _():
            xb_ref[...] = x_ref[...].astype(BF16)
            o_ref[...] = jnp.zeros_like(o_ref)

        x = xb_ref[...]
        hg = jnp.dot(x, wg_ref[...].astype(BF16), preferred_element_type=F32)
        hu = jnp.dot(x, wu_ref[...].astype(BF16), preferred_element_type=F32)
        h = hg * jax.nn.sigmoid(hg) * hu
        o_ref[...] += jnp.dot(h.astype(BF16), wd_ref[...].astype(BF16), preferred_element_type=F32)

    @pl.when((i >= nused_ref[0]) & (f == 0))
    def _():
        o_ref[...] = jnp.zeros_like(o_ref)


def experts_call(xs, tile_expert, n_used, wg, wu, wd, *, tm, tf=512, interpret=False):
    n_rows, D = xs.shape
    E, _, Fd = wg.shape
    tf = min(tf, Fd)
    nf = Fd // tf
    nt = n_rows // tm

    def tile(i, nu):
        return jnp.minimum(i, nu[0] - 1)

    def ftile(i, f, nu):
        return jnp.where(i < nu[0], f, nf - 1)

    return pl.pallas_call(
        _experts_kernel,
        grid_spec=pltpu.PrefetchScalarGridSpec(
            num_scalar_prefetch=2,
            grid=(nt, nf),
            in_specs=[pl.BlockSpec((tm, D), lambda i, f, te, nu: (tile(i, nu), 0)),
                      pl.BlockSpec((None, D, tf), lambda i, f, te, nu: (te[tile(i, nu)], 0, ftile(i, f, nu))),
                      pl.BlockSpec((None, D, tf), lambda i, f, te, nu: (te[tile(i, nu)], 0, ftile(i, f, nu))),
                      pl.BlockSpec((None, tf, D), lambda i, f, te, nu: (te[tile(i, nu)], ftile(i, f, nu), 0))],
            out_specs=pl.BlockSpec((tm, D), lambda i, f, te, nu: (i, 0)),
            scratch_shapes=[pltpu.VMEM((tm, D), BF16)]),
        out_shape=jax.ShapeDtypeStruct((n_rows, D), F32),
        compiler_params=_params(("arbitrary", "arbitrary"), 56),
        name="moe_experts",
        interpret=interpret,
    )(tile_expert, n_used, xs, wg, wu, wd)


def _combine_kernel(pos_ref, h_ref, info_ref, y_hbm, o_ref, buf_ref, sem, *, tm, n_tok):
    i = pl.program_id(0)
    n = pl.num_programs(0)

    def row_copies(tile, slot, r):
        t = tile * tm + r
        return (pltpu.make_async_copy(y_hbm.at[pl.ds(pos_ref[t], 1)], buf_ref.at[slot, 0, pl.ds(r, 1)],
                                      sem.at[slot]),
                pltpu.make_async_copy(y_hbm.at[pl.ds(pos_ref[n_tok + t], 1)], buf_ref.at[slot, 1, pl.ds(r, 1)],
                                      sem.at[slot]))

    def issue(tile, slot):
        def body(r, carry):
            for cp in row_copies(tile, slot, r):
                cp.start()
            return carry
        lax.fori_loop(0, tm, body, 0)

    @pl.when(i == 0)
    def _():
        issue(0, 0)

    @pl.when(i + 1 < n)
    def _():
        issue(i + 1, (i + 1) % 2)

    slot = i % 2

    def drain(r, carry):
        for cp in row_copies(i, slot, r):
            cp.wait()
        return carry
    lax.fori_loop(0, tm, drain, 0)

    info = info_ref[...]
    lane = lax.broadcasted_iota(jnp.int32, info.shape, 1)
    p1 = jnp.sum(jnp.where(lane == INFO_P1, info, 0.0), axis=-1, keepdims=True)
    p2 = jnp.sum(jnp.where(lane == INFO_P2, info, 0.0), axis=-1, keepdims=True)
    o_ref[...] = h_ref[...] + p1 * buf_ref[slot, 0] + p2 * buf_ref[slot, 1]


def combine_call(h, info, y, pos, *, tm=256, interpret=False):
    T, D = h.shape
    tm = min(tm, T)
    return pl.pallas_call(
        functools.partial(_combine_kernel, tm=tm, n_tok=T),
        grid_spec=pltpu.PrefetchScalarGridSpec(
            num_scalar_prefetch=1,
            grid=(T // tm,),
            in_specs=[pl.BlockSpec((tm, D), lambda i, pos: (i, 0)),
                      pl.BlockSpec((tm, LANES), lambda i, pos: (i, 0)),
                      pl.BlockSpec(memory_space=pl.ANY)],
            out_specs=pl.BlockSpec((tm, D), lambda i, pos: (i, 0)),
            scratch_shapes=[pltpu.VMEM((2, 2, tm, D), F32), pltpu.SemaphoreType.DMA((2,))]),
        out_shape=jax.ShapeDtypeStruct((T, D), F32),
        compiler_params=_params(("arbitrary",), 40),
        name="moe_combine",
        interpret=interpret,
    )(pos, h, info, y)


def _routing_tables(info, counts, n_experts, tm, n_tiles):
    cnt = counts[0, :n_experts].astype(jnp.int32)
    tiles_e = (cnt + tm - 1) // tm
    cum_tiles = jnp.cumsum(tiles_e)
    row_off = (cum_tiles - tiles_e) * tm
    experts = jnp.arange(n_experts, dtype=jnp.int32)

    def slot(e_lane, r_lane):
        e = info[:, e_lane].astype(jnp.int32)
        off = jnp.sum(jnp.where(e[:, None] == experts[None, :], row_off[None, :], 0), axis=-1)
        return off + info[:, r_lane].astype(jnp.int32)

    pos = jnp.concatenate([slot(INFO_E1, INFO_RANK1), slot(INFO_E2, INFO_RANK2)])
    tiles = jnp.arange(n_tiles, dtype=jnp.int32)
    tile_expert = jnp.minimum(jnp.sum(tiles[:, None] >= cum_tiles[None, :], axis=-1), n_experts - 1)
    return pos, tile_expert.astype(jnp.int32), cum_tiles[-1:].astype(jnp.int32)


def rwkv_layer(h, norm_g, mix, w_rkv, w0, w1, w2, a0, a1, a2, g1, g2, k_k, k_a, r_k, ln_w, ln_b, w_out,
               interpret=False):
    T, D = h.shape
    order = jnp.array([0, 2, 3, 1, 4, 5])
    xs = mixprep_call(h, norm_g, mix[order], interpret=interpret)
    rkv = matmul_call(xs, w_rkv, name="rwkv_rkv", interpret=interpret)
    g, logw, a = lora_call(xs, w0, w1, w2, a0, a1, a2, g1, g2, interpret=interpret)
    y, bonus = wkv7_bidir(rkv[0], rkv[1], rkv[2], logw, a, k_k, k_a, r_k.reshape(D), interpret=interpret)
    z = wkv_post_call(y, bonus, g, ln_w, ln_b, interpret=interpret)
    return matmul_call(z[None], w_out[None], h[None], name="rwkv_out", interpret=interpret)[0]


def attn_layer(h, norm_g, w_qkv, q_norm, k_norm, w_o, grid_w, interpret=False):
    T, D = h.shape
    q_heads = w_o.shape[0] // ATT_HEAD
    kv_heads = (w_qkv.shape[1] // ATT_HEAD - q_heads) // 2
    u = rms_norm_call(h, norm_g, BF16, interpret=interpret)
    qkv = matmul_call(u[None], w_qkv[None], name="attn_qkv", interpret=interpret)[0]
    cos, sin = _rope_tables(T, grid_w)
    q, k, v = qkprep_call(qkv, q_norm, k_norm, cos, sin, q_heads, kv_heads, interpret=interpret)
    o = flash_call(q, k, v, kv_heads, interpret=interpret)
    return matmul_call(o[None], w_o[None], h[None], name="attn_out", interpret=interpret)[0]


def ffn_layer(h, norm_g, wg, wu, wd, interpret=False):
    u = rms_norm_call(h, norm_g, BF16, interpret=interpret)
    return swiglu_call(u, wg, wu, wd, h, interpret=interpret)


MOE_TILE = 512
TOP_K = 2


def moe_layer(h, norm_g, w_router, wg, wu, wd, interpret=False):
    T, D = h.shape
    n_experts = w_router.shape[1]
    tm = min(MOE_TILE, T)
    n_tiles = TOP_K * T // tm + n_experts
    u, info, counts = router_call(h, norm_g, w_router, interpret=interpret)
    pos, tile_expert, n_used = _routing_tables(info, counts, n_experts, tm, n_tiles)
    xs = dispatch_call(u, pos, n_tiles * tm, interpret=interpret)
    ys = experts_call(xs, tile_expert, n_used, wg, wu, wd, tm=tm, interpret=interpret)
    return combine_call(h, info, ys, pos, interpret=interpret)


GRID_W = 64


def kernel(x, norm_mix_g, norm_ffn_g, rwkv_mix, rwkv_w_rkv, rwkv_w0, rwkv_w1, rwkv_w2, rwkv_a0, rwkv_a1, rwkv_a2, rwkv_g1, rwkv_g2, rwkv_k_k, rwkv_k_a, rwkv_r_k, rwkv_ln_w, rwkv_ln_b, rwkv_w_out, attn_w_qkv, attn_q_norm, attn_k_norm, attn_w_o, ffn_w_gate, ffn_w_up, ffn_w_down, moe_router, moe_w_gate, moe_w_up, moe_w_down, final_norm_g):
    B, T, D = x.shape
    depth = norm_mix_g.shape[0]
    outs = []
    for bi in range(B):
        h = x[bi]
        for i in range(depth):
            j = i // 2
            if i % 2 == 0:
                h = rwkv_layer(h, norm_mix_g[i], rwkv_mix[j], rwkv_w_rkv[j], rwkv_w0[j], rwkv_w1[j], rwkv_w2[j],
                               rwkv_a0[j], rwkv_a1[j], rwkv_a2[j], rwkv_g1[j], rwkv_g2[j], rwkv_k_k[j],
                               rwkv_k_a[j], rwkv_r_k[j], rwkv_ln_w[j], rwkv_ln_b[j], rwkv_w_out[j])
                h = ffn_layer(h, norm_ffn_g[i], ffn_w_gate[j], ffn_w_up[j], ffn_w_down[j])
            else:
                h = attn_layer(h, norm_mix_g[i], attn_w_qkv[j], attn_q_norm[j], attn_k_norm[j], attn_w_o[j], GRID_W)
                h = moe_layer(h, norm_ffn_g[i], moe_router[j], moe_w_gate[j], moe_w_up[j], moe_w_down[j])
        outs.append(rms_norm_call(h, final_norm_g, x.dtype))
    return jnp.stack(outs)
```

```python
import functools
import math

import jax
import jax.numpy as jnp
from jax import lax
from jax.experimental import pallas as pl
from jax.experimental.pallas import tpu as pltpu

F32 = jnp.float32
BF16 = jnp.bfloat16
HIGHEST = lax.Precision.HIGHEST

LANES = 128
RWKV_HEAD = 64
WKV_CHUNK = 64
GN_EPS = 64e-5
NORM_EPS = 1e-6


def _dot(a, b):
    return jnp.dot(a.astype(BF16), b.astype(BF16), preferred_element_type=F32)


def _dot_nt(a, b):
    return lax.dot_general(a.astype(BF16), b.astype(BF16), (((1,), (1,)), ((), ())),
                           preferred_element_type=F32)


def _dot_tn(a, b):
    return lax.dot_general(a.astype(BF16), b.astype(BF16), (((0,), (0,)), ((), ())),
                           preferred_element_type=F32)


def _dot_f32(a, b):
    return jnp.dot(a, b, preferred_element_type=F32, precision=HIGHEST)


def _dot_split(x, m, pieces, exact_lhs=False):
    mb = m.astype(BF16)
    acc = None
    rem = x
    for _ in range(pieces):
        part = rem.astype(BF16)
        rem = rem - part.astype(F32)
        t = (jnp.dot(mb, part, preferred_element_type=F32) if exact_lhs
             else jnp.dot(part, mb, preferred_element_type=F32))
        acc = t if acc is None else acc + t
    return acc


def _wkv_kernel(r_ref, k_ref, v_ref, lw_ref, a_ref, kk_ref, ka_ref, rk_ref,
                y_ref, bonus_ref, s_ref, *, pairs):
    L = WKV_CHUNK
    d = pl.program_id(0)
    c = pl.program_id(2)

    @pl.when(c == 0)
    def _():
        s_ref[...] = jnp.zeros_like(s_ref)

    sign = 1 - 2 * d
    row = lax.broadcasted_iota(jnp.int32, (L, 2 * L), 0)
    col = lax.broadcasted_iota(jnp.int32, (L, 2 * L), 1)
    order = (row - (col & (L - 1))) * sign
    strict = order > 0
    incl = order >= 0
    first_head_cols = col < L
    r2 = lax.broadcasted_iota(jnp.int32, (L, L), 0)
    c2 = lax.broadcasted_iota(jnp.int32, (L, L), 1)
    tri = jnp.where((r2 - c2) * sign >= 0, 1.0, 0.0).astype(F32)
    lane = lax.broadcasted_iota(jnp.int32, (L, LANES), 1)
    first_head = lane < RWKV_HEAD
    hr = lax.broadcasted_iota(jnp.int32, (LANES, LANES), 0)
    hc = lax.broadcasted_iota(jnp.int32, (LANES, LANES), 1)
    head_ones = jnp.where((hr < RWKV_HEAD) == (hc < RWKV_HEAD), 1.0, 0.0).astype(F32)

    def stack(x):
        return jnp.concatenate([jnp.where(first_head, x, 0.0), jnp.where(first_head, 0.0, x)], axis=0)

    def block_diag(m):
        return jnp.concatenate([jnp.where(first_head_cols, m, 0.0), jnp.where(first_head_cols, 0.0, m)], axis=0)

    P = range(pairs)
    sls = [slice(p * LANES, (p + 1) * LANES) for p in P]
    each = lambda fn: [fn(p) for p in P]
    r = each(lambda p: r_ref[:, sls[p]])
    k = each(lambda p: k_ref[:, sls[p]])
    v = each(lambda p: v_ref[:, sls[p]])
    lw = each(lambda p: lw_ref[0, :, sls[p]])
    a = each(lambda p: a_ref[0, :, sls[p]])

    kkr = each(lambda p: k[p] * kk_ref[:, sls[p]])
    ss = each(lambda p: _dot_split(kkr[p] * kkr[p], head_ones, 2))
    cs = each(lambda p: _dot_split(lw[p], tri, 3, exact_lhs=True))
    kd = each(lambda p: k[p] * (1.0 + (a[p] - 1.0) * ka_ref[:, sls[p]]))
    bsum = each(lambda p: _dot_split(r[p] * kd[p] * rk_ref[:, sls[p]], head_ones, 2))
    for p in P:
        bonus_ref[0, :, sls[p]] = bsum[p] * v[p]
    kk = each(lambda p: kkr[p] / jnp.maximum(jnp.sqrt(ss[p]), 1e-12))
    b = each(lambda p: kk[p] * a[p])
    tot = each(lambda p: jnp.sum(lw[p], axis=0, keepdims=True))
    a_t = each(lambda p: -kk[p] * jnp.exp(cs[p] - lw[p]))
    r_t = each(lambda p: r[p] * jnp.exp(cs[p]))
    g_inv = each(lambda p: jnp.exp(-cs[p]))
    g_rem = each(lambda p: jnp.exp(tot[p] - cs[p]))

    sc = each(lambda p: _dot_nt(jnp.concatenate([a_t[p], r_t[p]], axis=0),
                                jnp.concatenate([stack(b[p] * g_inv[p]), stack(kd[p] * g_inv[p])], axis=0)))
    m_ak = each(lambda p: jnp.where(strict, sc[p][:L, 2 * L:], 0.0))
    m_rb = each(lambda p: jnp.where(incl, sc[p][L:, :2 * L], 0.0))
    m_rk = each(lambda p: jnp.where(incl, sc[p][L:, 2 * L:], 0.0))

    pw = each(lambda p: block_diag(jnp.where(strict, sc[p][:L, :2 * L], 0.0)))
    n = pw
    for _ in range(int(math.log2(L)) - 1):
        pw = each(lambda p: _dot(pw[p], pw[p]))
        n = each(lambda p: n[p] + pw[p] + _dot(n[p], pw[p]))

    s = each(lambda p: s_ref[p])
    vs = each(lambda p: stack(v[p]))
    x = each(lambda p: _dot_nt(a_t[p], s[p]) + _dot(m_ak[p], vs[p]))
    xs = each(lambda p: stack(x[p]))
    us = each(lambda p: xs[p] + _dot(n[p], xs[p]))
    for p in P:
        y_ref[0, :, sls[p]] = _dot_nt(r_t[p], s[p]) + _dot(m_rb[p], us[p]) + _dot(m_rk[p], vs[p])
    for p in P:
        s_ref[p] = s[p] * jnp.exp(tot[p]) + _dot_tn(
            jnp.concatenate([us[p], vs[p]], axis=0),
            jnp.concatenate([stack(b[p] * g_rem[p]), stack(kd[p] * g_rem[p])], axis=0))


def wkv7_bidir(r, k, v, logw, a, k_k, k_a, r_k, *, pairs=8, interpret=False):
    T, D = r.shape
    L = WKV_CHUNK
    nc = T // L
    pairs = min(pairs, D // LANES)
    W = LANES * pairs
    nj = D // W
    row_map = lambda d, j, c: (c + d * (nc - 1 - 2 * c), j)
    dir_map = lambda d, j, c: (d, c + d * (nc - 1 - 2 * c), j)
    par_map = lambda d, j, c: (0, j)
    tok = pl.BlockSpec((L, W), row_map)
    dtok = pl.BlockSpec((1, L, W), dir_map)
    par = pl.BlockSpec((1, W), par_map)
    return pl.pallas_call(
        functools.partial(_wkv_kernel, pairs=pairs),
        grid=(2, nj, nc),
        in_specs=[tok, tok, tok, dtok, dtok, par, par, par],
        out_specs=[dtok, dtok],
        out_shape=[jax.ShapeDtypeStruct((2, T, D), F32)] * 2,
        scratch_shapes=[pltpu.VMEM((pairs, LANES, LANES), F32)],
        compiler_params=pltpu.CompilerParams(dimension_semantics=("arbitrary", "arbitrary", "arbitrary")),
        name="wkv7_bidir",
        interpret=interpret,
    )(r, k, v, logw, a, k_k.reshape(1, D), k_a.reshape(1, D), r_k.reshape(1, D))


def _params(sem, vmem_mb):
    return pltpu.CompilerParams(dimension_semantics=sem, vmem_limit_bytes=vmem_mb * 1024 * 1024)


def _rms(x, g):
    return x * lax.rsqrt(jnp.mean(x * x, axis=-1, keepdims=True) + NORM_EPS) * g


def _norm_kernel(h_ref, g_ref, o_ref):
    o_ref[...] = _rms(h_ref[...], g_ref[...]).astype(o_ref.dtype)


def rms_norm_call(h, g, out_dtype, *, tm=256, interpret=False):
    T, D = h.shape
    tm = min(tm, T)
    return pl.pallas_call(
        _norm_kernel,
        grid=(T // tm,),
        in_specs=[pl.BlockSpec((tm, D), lambda i: (i, 0)), pl.BlockSpec((1, D), lambda i: (0, 0))],
        out_specs=pl.BlockSpec((tm, D), lambda i: (i, 0)),
        out_shape=jax.ShapeDtypeStruct((T, D), out_dtype),
        compiler_params=_params(("arbitrary",), 32),
        name="rms_norm",
        interpret=interpret,
    )(h, g.reshape(1, D))


HALO = 8


def _mixprep_kernel(h_ref, hp_ref, hn_ref, g_ref, mix_ref, o_ref):
    i = pl.program_id(0)
    last = pl.num_programs(0) - 1
    g = g_ref[...]
    u = _rms(h_ref[...], g)
    tm = u.shape[0]
    up = jnp.where(i == 0, 0.0, _rms(hp_ref[HALO - 1:HALO, :], g))
    un = jnp.where(i == last, 0.0, _rms(hn_ref[0:1, :], g))
    rows = lax.broadcasted_iota(jnp.int32, u.shape, 0)
    prev = jnp.where(rows == 0, up, pltpu.roll(u, 1, 0))
    nxt = jnp.where(rows == tm - 1, un, pltpu.roll(u, tm - 1, 0))
    xx = 0.5 * (prev + nxt) - u
    for m in range(o_ref.shape[0]):
        o_ref[m] = (u + xx * mix_ref[m:m + 1, :]).astype(o_ref.dtype)


def mixprep_call(h, g, mix, *, tm=256, interpret=False):
    T, D = h.shape
    tm = min(tm, T)
    nm = mix.shape[0]
    nb = tm // HALO
    nh = T // HALO
    return pl.pallas_call(
        _mixprep_kernel,
        grid=(T // tm,),
        in_specs=[pl.BlockSpec((tm, D), lambda i: (i, 0)),
                  pl.BlockSpec((HALO, D), lambda i: (jnp.maximum(i * nb - 1, 0), 0)),
                  pl.BlockSpec((HALO, D), lambda i: (jnp.minimum((i + 1) * nb, nh - 1), 0)),
                  pl.BlockSpec((1, D), lambda i: (0, 0)),
                  pl.BlockSpec((nm, D), lambda i: (0, 0))],
        out_specs=pl.BlockSpec((nm, tm, D), lambda i: (0, i, 0)),
        out_shape=jax.ShapeDtypeStruct((nm, T, D), BF16),
        compiler_params=_params(("arbitrary",), 48),
        name="rwkv_mixprep",
        interpret=interpret,
    )(h, h, h, g.reshape(1, D), mix)


def _mm_kernel(*refs, has_resid):
    if has_resid:
        x_ref, w_ref, r_ref, o_ref, wb_ref = refs
    else:
        x_ref, w_ref, o_ref, wb_ref = refs

    @pl.when(pl.program_id(2) == 0)
    def _():
        wb_ref[...] = w_ref[...].astype(BF16)

    acc = jnp.dot(x_ref[...], wb_ref[...], preferred_element_type=F32)
    if has_resid:
        acc = acc + r_ref[...]
    o_ref[...] = acc.astype(o_ref.dtype)


def matmul_call(x, w, resid=None, *, out_dtype=F32, tm=512, tn=512, name="matmul", interpret=False):
    _, T, Kd = x.shape
    S, _, N = w.shape
    tm, tn = min(tm, T), min(tn, N)
    in_specs = [pl.BlockSpec((None, tm, Kd), lambda s, j, i: (s, i, 0)),
                pl.BlockSpec((None, Kd, tn), lambda s, j, i: (s, 0, j))]
    args = [x, w]
    if resid is not None:
        in_specs.append(pl.BlockSpec((None, tm, tn), lambda s, j, i: (s, i, j)))
        args.append(resid)
    return pl.pallas_call(
        functools.partial(_mm_kernel, has_resid=resid is not None),
        grid=(S, N // tn, T // tm),
        in_specs=in_specs,
        out_specs=pl.BlockSpec((None, tm, tn), lambda s, j, i: (s, i, j)),
        out_shape=jax.ShapeDtypeStruct((S, T, N), out_dtype),
        scratch_shapes=[pltpu.VMEM((Kd, tn), BF16)],
        compiler_params=_params(("arbitrary", "arbitrary", "arbitrary"), 48),
        name=name,
        interpret=interpret,
    )(*args)


def _lora_kernel(xw_ref, xa_ref, xg_ref, w1_ref, w2_ref, w0_ref, a1_ref, a2_ref, a0_ref, g1_ref, g2_ref,
                 g_ref, lw_ref, a_ref):
    xw = xw_ref[...]
    xa = xa_ref[...]
    g_ref[...] = _dot(jax.nn.sigmoid(_dot(xg_ref[...], g1_ref[...])), g2_ref[...])
    for d in range(2):
        wl = w0_ref[d:d + 1, :] + _dot(jnp.tanh(_dot(xw, w1_ref[d])), w2_ref[d])
        lw_ref[d] = -jnp.exp(-jax.nn.softplus(-wl) - 0.5)
        a_ref[d] = jax.nn.sigmoid(a0_ref[d:d + 1, :] + _dot(_dot(xa, a1_ref[d]), a2_ref[d]))


def _pad_to(x, axis, mult):
    pad = (-x.shape[axis]) % mult
    if pad == 0:
        return x
    widths = [(0, 0)] * x.ndim
    widths[axis] = (0, pad)
    return jnp.pad(x, widths)


def lora_call(xs, w0, w1, w2, a0, a1, a2, g1, g2, *, tm=256, interpret=False):
    _, T, D = xs.shape
    tm = min(tm, T)
    w1 = _pad_to(w1, 2, LANES).astype(BF16)
    w2 = _pad_to(w2, 1, LANES).astype(BF16)
    a1 = _pad_to(a1, 2, LANES).astype(BF16)
    a2 = _pad_to(a2, 1, LANES).astype(BF16)
    g1 = g1.astype(BF16)
    g2 = g2.astype(BF16)
    xspec = lambda m: pl.BlockSpec((None, tm, D), lambda i: (m, i, 0))
    full = lambda a: pl.BlockSpec(a.shape, lambda i: (0,) * a.ndim)
    return pl.pallas_call(
        _lora_kernel,
        grid=(T // tm,),
        in_specs=[xspec(3), xspec(4), xspec(5), full(w1), full(w2), full(w0), full(a1), full(a2), full(a0),
                  full(g1), full(g2)],
        out_specs=[pl.BlockSpec((tm, D), lambda i: (i, 0)),
                   pl.BlockSpec((2, tm, D), lambda i: (0, i, 0)),
                   pl.BlockSpec((2, tm, D), lambda i: (0, i, 0))],
        out_shape=[jax.ShapeDtypeStruct((T, D), F32), jax.ShapeDtypeStruct((2, T, D), F32),
                   jax.ShapeDtypeStruct((2, T, D), F32)],
        compiler_params=_params(("arbitrary",), 56),
        name="rwkv_lora",
        interpret=interpret,
    )(xs, xs, xs, w1, w2, w0, a1, a2, a0, g1, g2)


def _wkv_post_kernel(y_ref, b_ref, g_ref, lnw_ref, lnb_ref, o_ref):
    hr = lax.broadcasted_iota(jnp.int32, (LANES, LANES), 0)
    hc = lax.broadcasted_iota(jnp.int32, (LANES, LANES), 1)
    head_mean = jnp.where((hr < RWKV_HEAD) == (hc < RWKV_HEAD), 1.0 / RWKV_HEAD, 0.0).astype(F32)
    for p in range(o_ref.shape[1] // LANES):
        sl = slice(p * LANES, (p + 1) * LANES)
        y = y_ref[0, :, sl] + y_ref[1, :, sl]
        yc = y - _dot_f32(y, head_mean)
        var = _dot_f32(yc * yc, head_mean)
        yn = yc * lax.rsqrt(var + GN_EPS)
        out = yn * lnw_ref[:, sl] + lnb_ref[:, sl] + b_ref[0, :, sl] + b_ref[1, :, sl]
        o_ref[:, sl] = (out * g_ref[:, sl]).astype(o_ref.dtype)


def wkv_post_call(y, bonus, g, ln_w, ln_b, *, tm=256, interpret=False):
    _, T, D = y.shape
    tm = min(tm, T)
    two = pl.BlockSpec((2, tm, D), lambda i: (0, i, 0))
    one = pl.BlockSpec((tm, D), lambda i: (i, 0))
    par = pl.BlockSpec((1, D), lambda i: (0, 0))
    return pl.pallas_call(
        _wkv_post_kernel,
        grid=(T // tm,),
        in_specs=[two, two, one, par, par],
        out_specs=one,
        out_shape=jax.ShapeDtypeStruct((T, D), BF16),
        compiler_params=_params(("arbitrary",), 48),
        name="wkv_post",
        interpret=interpret,
    )(y, bonus, g, ln_w.reshape(1, D), ln_b.reshape(1, D))


def _swiglu_kernel(x_ref, wg_ref, wu_ref, wd_ref, r_ref, o_ref):
    @pl.when(pl.program_id(1) == 0)
    def _():
        o_ref[...] = r_ref[...]

    x = x_ref[...]
    hg = jnp.dot(x, wg_ref[...].astype(BF16), preferred_element_type=F32)
    hu = jnp.dot(x, wu_ref[...].astype(BF16), preferred_element_type=F32)
    h = hg * jax.nn.sigmoid(hg) * hu
    o_ref[...] += jnp.dot(h.astype(BF16), wd_ref[...].astype(BF16), preferred_element_type=F32)


def swiglu_call(x, wg, wu, wd, resid, *, tm=512, tf=512, interpret=False):
    T, D = x.shape
    Fd = wg.shape[1]
    tm, tf = min(tm, T), min(tf, Fd)
    return pl.pallas_call(
        _swiglu_kernel,
        grid=(T // tm, Fd // tf),
        in_specs=[pl.BlockSpec((tm, D), lambda i, f: (i, 0)),
                  pl.BlockSpec((D, tf), lambda i, f: (0, f)),
                  pl.BlockSpec((D, tf), lambda i, f: (0, f)),
                  pl.BlockSpec((tf, D), lambda i, f: (f, 0)),
                  pl.BlockSpec((tm, D), lambda i, f: (i, 0))],
        out_specs=pl.BlockSpec((tm, D), lambda i, f: (i, 0)),
        out_shape=jax.ShapeDtypeStruct((T, D), F32),
        compiler_params=_params(("arbitrary", "arbitrary"), 56),
        name="swiglu",
        interpret=interpret,
    )(x, wg, wu, wd, resid)


ATT_HEAD = 128
ROPE_HALF = 32


def _qkprep_kernel(qkv_ref, qn_ref, kn_ref, cos_ref, sin_ref, q_ref, k_ref, v_ref, *, q_heads, kv_heads):
    cos = cos_ref[...]
    sin = sin_ref[...]
    lane = lax.broadcasted_iota(jnp.int32, cos.shape, 1)
    low = (lane & (2 * ROPE_HALF - 1)) < ROPE_HALF
    scale = math.log2(math.e) / math.sqrt(ATT_HEAD)

    def head(x, g):
        xn = _rms(x, g)
        partner = jnp.where(low, pltpu.roll(xn, ATT_HEAD - ROPE_HALF, 1), pltpu.roll(xn, ROPE_HALF, 1))
        return xn * cos + partner * sin

    for hh in range(q_heads):
        sl = slice(hh * ATT_HEAD, (hh + 1) * ATT_HEAD)
        q_ref[:, sl] = (head(qkv_ref[:, sl], qn_ref[...]) * scale).astype(q_ref.dtype)
    for hh in range(kv_heads):
        src = slice((q_heads + hh) * ATT_HEAD, (q_heads + hh + 1) * ATT_HEAD)
        dst = slice(hh * ATT_HEAD, (hh + 1) * ATT_HEAD)
        k_ref[:, dst] = head(qkv_ref[:, src], kn_ref[...]).astype(k_ref.dtype)
    v0 = (q_heads + kv_heads) * ATT_HEAD
    v_ref[...] = qkv_ref[:, v0:].astype(v_ref.dtype)


def qkprep_call(qkv, q_norm, k_norm, cos, sin, q_heads, kv_heads, *, tm=256, interpret=False):
    T, Wd = qkv.shape
    tm = min(tm, T)
    qw, kw = q_heads * ATT_HEAD, kv_heads * ATT_HEAD
    row = lambda w: pl.BlockSpec((tm, w), lambda i: (i, 0))
    par = pl.BlockSpec((1, ATT_HEAD), lambda i: (0, 0))
    return pl.pallas_call(
        functools.partial(_qkprep_kernel, q_heads=q_heads, kv_heads=kv_heads),
        grid=(T // tm,),
        in_specs=[row(Wd), par, par, row(ATT_HEAD), row(ATT_HEAD)],
        out_specs=[row(qw), row(kw), row(kw)],
        out_shape=[jax.ShapeDtypeStruct((T, qw), BF16), jax.ShapeDtypeStruct((T, kw), BF16),
                   jax.ShapeDtypeStruct((T, kw), BF16)],
        compiler_params=_params(("arbitrary",), 32),
        name="attn_qkprep",
        interpret=interpret,
    )(qkv, q_norm.reshape(1, ATT_HEAD), k_norm.reshape(1, ATT_HEAD), cos, sin)


def _rope_tables(T, grid_w):
    pos = jnp.arange(T)
    row_ids = (pos // grid_w).astype(F32)
    col_ids = (pos % grid_w).astype(F32)
    axis_dim = ATT_HEAD // 2
    inv = 10000.0 ** (-jnp.arange(0, axis_dim, 2, dtype=F32) / axis_dim)
    ang_r = row_ids[:, None] * inv[None, :]
    ang_c = col_ids[:, None] * inv[None, :]
    cos = jnp.concatenate([jnp.cos(ang_r), jnp.cos(ang_r), jnp.cos(ang_c), jnp.cos(ang_c)], axis=-1)
    sin = jnp.concatenate([-jnp.sin(ang_r), jnp.sin(ang_r), -jnp.sin(ang_c), jnp.sin(ang_c)], axis=-1)
    return cos, sin


def _flash_kernel(q_ref, k_ref, v_ref, o_ref, m_ref, l_ref, acc_ref, *, rep):
    kv = pl.program_id(2)

    @pl.when(kv == 0)
    def _():
        m_ref[...] = jnp.full_like(m_ref, -jnp.inf)
        l_ref[...] = jnp.zeros_like(l_ref)
        acc_ref[...] = jnp.zeros_like(acc_ref)

    k = k_ref[...]
    v = v_ref[...]
    for hh in range(rep):
        sl = slice(hh * ATT_HEAD, (hh + 1) * ATT_HEAD)
        s = lax.dot_general(q_ref[:, sl], k, (((1,), (1,)), ((), ())), preferred_element_type=F32)
        m_prev = m_ref[hh]
        m_new = jnp.maximum(m_prev, jnp.max(s, axis=-1, keepdims=True))
        alpha = jnp.exp2(m_prev - m_new)
        p = jnp.exp2(s - m_new[:, :1])
        l_ref[hh] = alpha * l_ref[hh] + jnp.sum(p, axis=-1, keepdims=True)
        m_ref[hh] = m_new
        acc_ref[:, sl] = alpha * acc_ref[:, sl] + jnp.dot(p.astype(BF16), v, preferred_element_type=F32)

    @pl.when(kv == pl.num_programs(2) - 1)
    def _():
        for hh in range(rep):
            sl = slice(hh * ATT_HEAD, (hh + 1) * ATT_HEAD)
            o_ref[:, sl] = (acc_ref[:, sl] / l_ref[hh]).astype(o_ref.dtype)


def flash_call(q, k, v, kv_heads, *, tq=512, tk=8192, interpret=False):
    T, qw = q.shape
    rep = qw // ATT_HEAD // kv_heads
    tq, tk = min(tq, T), min(tk, T)
    gw = rep * ATT_HEAD
    return pl.pallas_call(
        functools.partial(_flash_kernel, rep=rep),
        grid=(kv_heads, T // tq, T // tk),
        in_specs=[pl.BlockSpec((tq, gw), lambda g, i, j: (i, g)),
                  pl.BlockSpec((tk, ATT_HEAD), lambda g, i, j: (j, g)),
                  pl.BlockSpec((tk, ATT_HEAD), lambda g, i, j: (j, g))],
        out_specs=pl.BlockSpec((tq, gw), lambda g, i, j: (i, g)),
        out_shape=jax.ShapeDtypeStruct((T, qw), BF16),
        scratch_shapes=[pltpu.VMEM((rep, tq, ATT_HEAD), F32), pltpu.VMEM((rep, tq, ATT_HEAD), F32),
                        pltpu.VMEM((tq, gw), F32)],
        compiler_params=_params(("arbitrary", "arbitrary", "arbitrary"), 48),
        name="gqa_flash",
        interpret=interpret,
    )(q, k, v)


(INFO_E1, INFO_E2, INFO_RANK1, INFO_RANK2, INFO_P1, INFO_P2) = range(6)


def _router_kernel(h_ref, g_ref, wr_ref, u_ref, info_ref, cnt_ref, *, n_experts):
    @pl.when(pl.program_id(0) == 0)
    def _():
        cnt_ref[...] = jnp.zeros_like(cnt_ref)

    u = _rms(h_ref[...], g_ref[...])
    u_ref[...] = u
    logits = _dot_f32(u, wr_ref[...])
    lane = lax.broadcasted_iota(jnp.int32, logits.shape, 1)
    logits = jnp.where(lane < n_experts, logits, -jnp.inf)
    m1 = jnp.max(logits, axis=-1, keepdims=True)
    i1 = jnp.min(jnp.where(logits == m1, lane, LANES), axis=-1, keepdims=True)
    rest = jnp.where(lane == i1, -jnp.inf, logits)
    m2 = jnp.max(rest, axis=-1, keepdims=True)
    i2 = jnp.min(jnp.where(rest == m2, lane, LANES), axis=-1, keepdims=True)
    e2 = jnp.exp(m2 - m1)
    p1 = 1.0 / (1.0 + e2)
    p2 = e2 / (1.0 + e2)

    oh1 = jnp.where(lane == i1, 1.0, 0.0)
    oh2 = jnp.where(lane == i2, 1.0, 0.0)
    both = oh1 + oh2
    tm = both.shape[0]
    tr = lax.broadcasted_iota(jnp.int32, (tm, tm), 0)
    tc = lax.broadcasted_iota(jnp.int32, (tm, tm), 1)
    cum = _dot(jnp.where(tr >= tc, 1.0, 0.0), both)
    before = cum - both + cnt_ref[...]
    rank1 = jnp.sum(oh1 * before, axis=-1, keepdims=True)
    rank2 = jnp.sum(oh2 * before, axis=-1, keepdims=True)
    cnt_ref[...] += cum[tm - 1:tm, :]

    info = jnp.zeros_like(logits)
    for idx, val in ((INFO_E1, i1.astype(F32)), (INFO_E2, i2.astype(F32)), (INFO_RANK1, rank1),
                     (INFO_RANK2, rank2), (INFO_P1, p1), (INFO_P2, p2)):
        info = jnp.where(lane == idx, val, info)
    info_ref[...] = info


def router_call(h, g, w_router, *, tm=256, interpret=False):
    T, D = h.shape
    tm = min(tm, T)
    n_experts = w_router.shape[1]
    wr = _pad_to(w_router, 1, LANES)
    return pl.pallas_call(
        functools.partial(_router_kernel, n_experts=n_experts),
        grid=(T // tm,),
        in_specs=[pl.BlockSpec((tm, D), lambda i: (i, 0)), pl.BlockSpec((1, D), lambda i: (0, 0)),
                  pl.BlockSpec((D, LANES), lambda i: (0, 0))],
        out_specs=[pl.BlockSpec((tm, D), lambda i: (i, 0)), pl.BlockSpec((tm, LANES), lambda i: (i, 0)),
                   pl.BlockSpec((1, LANES), lambda i: (0, 0))],
        out_shape=[jax.ShapeDtypeStruct((T, D), F32), jax.ShapeDtypeStruct((T, LANES), F32),
                   jax.ShapeDtypeStruct((1, LANES), F32)],
        compiler_params=_params(("arbitrary",), 32),
        name="moe_router",
        interpret=interpret,
    )(h, g.reshape(1, D), wr)


def _dispatch_kernel(pos_ref, u_ref, xs_init_hbm, xs_hbm, sem, *, tm, n_tok):
    del xs_init_hbm
    base = pl.program_id(0) * tm

    def row_copies(r):
        src = u_ref.at[pl.ds(r, 1)]
        t = base + r
        return (pltpu.make_async_copy(src, xs_hbm.at[pl.ds(pos_ref[t], 1)], sem),
                pltpu.make_async_copy(src, xs_hbm.at[pl.ds(pos_ref[n_tok + t], 1)], sem))

    def issue(r, carry):
        for cp in row_copies(r):
            cp.start()
        return carry

    def drain(r, carry):
        for cp in row_copies(r):
            cp.wait()
        return carry

    lax.fori_loop(0, tm, issue, 0)
    lax.fori_loop(0, tm, drain, 0)


def dispatch_call(u, pos, n_rows, *, tm=256, interpret=False):
    T, D = u.shape
    tm = min(tm, T)
    xs0 = jnp.zeros((n_rows, D), u.dtype)
    return pl.pallas_call(
        functools.partial(_dispatch_kernel, tm=tm, n_tok=T),
        grid_spec=pltpu.PrefetchScalarGridSpec(
            num_scalar_prefetch=1,
            grid=(T // tm,),
            in_specs=[pl.BlockSpec((tm, D), lambda i, pos: (i, 0)), pl.BlockSpec(memory_space=pl.ANY)],
            out_specs=pl.BlockSpec(memory_space=pl.ANY),
            scratch_shapes=[pltpu.SemaphoreType.DMA(())]),
        out_shape=jax.ShapeDtypeStruct((n_rows, D), u.dtype),
        input_output_aliases={2: 0},
        compiler_params=_params(("arbitrary",), 32),
        name="moe_dispatch",
        interpret=interpret,
    )(pos, u, xs0)


def _experts_kernel(te_ref, nused_ref, x_ref, wg_ref, wu_ref, wd_ref, o_ref, xb_ref):
    del te_ref
    i = pl.program_id(0)
    f = pl.program_id(1)

    @pl.when(i < nused_ref[0])
    def _():
        @pl.when(f == 0)
        def _():
            xb_ref[...] = x_ref[...].astype(BF16)
            o_ref[...] = jnp.zeros_like(o_ref)

        x = xb_ref[...]
        hg = jnp.dot(x, wg_ref[...].astype(BF16), preferred_element_type=F32)
        hu = jnp.dot(x, wu_ref[...].astype(BF16), preferred_element_type=F32)
        h = hg * jax.nn.sigmoid(hg) * hu
        o_ref[...] += jnp.dot(h.astype(BF16), wd_ref[...].astype(BF16), preferred_element_type=F32)

    @pl.when((i >= nused_ref[0]) & (f == 0))
    def _():
        o_ref[...] = jnp.zeros_like(o_ref)


def experts_call(xs, tile_expert, n_used, wg, wu, wd, *, tm, tf=512, interpret=False):
    n_rows, D = xs.shape
    E, _, Fd = wg.shape
    tf = min(tf, Fd)
    nf = Fd // tf
    nt = n_rows // tm

    def tile(i, nu):
        return jnp.maximum(jnp.minimum(i, nu[0] - 1), 0)

    def ftile(i, f, nu):
        return jnp.where(i < nu[0], f, nf - 1)

    return pl.pallas_call(
        _experts_kernel,
        grid_spec=pltpu.PrefetchScalarGridSpec(
            num_scalar_prefetch=2,
            grid=(nt, nf),
            in_specs=[pl.BlockSpec((tm, D), lambda i, f, te, nu: (tile(i, nu), 0)),
                      pl.BlockSpec((None, D, tf), lambda i, f, te, nu: (te[tile(i, nu)], 0, ftile(i, f, nu))),
                      pl.BlockSpec((None, D, tf), lambda i, f, te, nu: (te[tile(i, nu)], 0, ftile(i, f, nu))),
                      pl.BlockSpec((None, tf, D), lambda i, f, te, nu: (te[tile(i, nu)], ftile(i, f, nu), 0))],
            out_specs=pl.BlockSpec((tm, D), lambda i, f, te, nu: (i, 0)),
            scratch_shapes=[pltpu.VMEM((tm, D), BF16)]),
        out_shape=jax.ShapeDtypeStruct((n_rows, D), F32),
        compiler_params=_params(("arbitrary", "arbitrary"), 56),
        name="moe_experts",
        interpret=interpret,
    )(tile_expert, n_used, xs, wg, wu, wd)


def _combine_kernel(pos_ref, h_ref, info_ref, y_hbm, o_ref, buf_ref, sem, *, tm, n_tok):
    i = pl.program_id(0)
    n = pl.num_programs(0)

    def row_copies(tile, slot, r):
        t = tile * tm + r
        return (pltpu.make_async_copy(y_hbm.at[pl.ds(pos_ref[t], 1)], buf_ref.at[slot, 0, pl.ds(r, 1)],
                                      sem.at[slot]),
                pltpu.make_async_copy(y_hbm.at[pl.ds(pos_ref[n_tok + t], 1)], buf_ref.at[slot, 1, pl.ds(r, 1)],
                                      sem.at[slot]))

    def issue(tile, slot):
        def body(r, carry):
            for cp in row_copies(tile, slot, r):
                cp.start()
            return carry
        lax.fori_loop(0, tm, body, 0)

    @pl.when(i == 0)
    def _():
        issue(0, 0)

    @pl.when(i + 1 < n)
    def _():
        issue(i + 1, (i + 1) % 2)

    slot = i % 2

    def drain(r, carry):
        for cp in row_copies(i, slot, r):
            cp.wait()
        return carry
    lax.fori_loop(0, tm, drain, 0)

    info = info_ref[...]
    lane = lax.broadcasted_iota(jnp.int32, info.shape, 1)
    p1 = jnp.sum(jnp.where(lane == INFO_P1, info, 0.0), axis=-1, keepdims=True)
    p2 = jnp.sum(jnp.where(lane == INFO_P2, info, 0.0), axis=-1, keepdims=True)
    o_ref[...] = h_ref[...] + p1 * buf_ref[slot, 0] + p2 * buf_ref[slot, 1]


def combine_call(h, info, y, pos, *, tm=256, interpret=False):
    T, D = h.shape
    tm = min(tm, T)
    return pl.pallas_call(
        functools.partial(_combine_kernel, tm=tm, n_tok=T),
        grid_spec=pltpu.PrefetchScalarGridSpec(
            num_scalar_prefetch=1,
            grid=(T // tm,),
            in_specs=[pl.BlockSpec((tm, D), lambda i, pos: (i, 0)),
                      pl.BlockSpec((tm, LANES), lambda i, pos: (i, 0)),
                      pl.BlockSpec(memory_space=pl.ANY)],
            out_specs=pl.BlockSpec((tm, D), lambda i, pos: (i, 0)),
            scratch_shapes=[pltpu.VMEM((2, 2, tm, D), F32), pltpu.SemaphoreType.DMA((2,))]),
        out_shape=jax.ShapeDtypeStruct((T, D), F32),
        compiler_params=_params(("arbitrary",), 40),
        name="moe_combine",
        interpret=interpret,
    )(pos, h, info, y)


def _routing_tables(info, counts, n_experts, tm, n_tiles):
    cnt = counts[0, :n_experts].astype(jnp.int32)
    tiles_e = (cnt + tm - 1) // tm
    cum_tiles = jnp.cumsum(tiles_e)
    row_off = (cum_tiles - tiles_e) * tm
    experts = jnp.arange(n_experts, dtype=jnp.int32)

    def slot(e_lane, r_lane):
        e = info[:, e_lane].astype(jnp.int32)
        off = jnp.sum(jnp.where(e[:, None] == experts[None, :], row_off[None, :], 0), axis=-1)
        return off + info[:, r_lane].astype(jnp.int32)

    pos = jnp.concatenate([slot(INFO_E1, INFO_RANK1), slot(INFO_E2, INFO_RANK2)])
    tiles = jnp.arange(n_tiles, dtype=jnp.int32)
    tile_expert = jnp.minimum(jnp.sum(tiles[:, None] >= cum_tiles[None, :], axis=-1), n_experts - 1)
    return pos, tile_expert.astype(jnp.int32), cum_tiles[-1:].astype(jnp.int32)


def rwkv_layer(h, norm_g, mix, w_rkv, w0, w1, w2, a0, a1, a2, g1, g2, k_k, k_a, r_k, ln_w, ln_b, w_out,
               interpret=False):
    T, D = h.shape
    order = jnp.array([0, 2, 3, 1, 4, 5])
    xs = mixprep_call(h, norm_g, mix[order], interpret=interpret)
    rkv = matmul_call(xs, w_rkv, name="rwkv_rkv", interpret=interpret)
    g, logw, a = lora_call(xs, w0, w1, w2, a0, a1, a2, g1, g2, interpret=interpret)
    y, bonus = wkv7_bidir(rkv[0], rkv[1], rkv[2], logw, a, k_k, k_a, r_k.reshape(D), interpret=interpret)
    z = wkv_post_call(y, bonus, g, ln_w, ln_b, interpret=interpret)
    return matmul_call(z[None], w_out[None], h[None], name="rwkv_out", interpret=interpret)[0]


def attn_layer(h, norm_g, w_qkv, q_norm, k_norm, w_o, grid_w, interpret=False):
    T, D = h.shape
    q_heads = w_o.shape[0] // ATT_HEAD
    kv_heads = (w_qkv.shape[1] // ATT_HEAD - q_heads) // 2
    u = rms_norm_call(h, norm_g, BF16, interpret=interpret)
    qkv = matmul_call(u[None], w_qkv[None], name="attn_qkv", interpret=interpret)[0]
    cos, sin = _rope_tables(T, grid_w)
    q, k, v = qkprep_call(qkv, q_norm, k_norm, cos, sin, q_heads, kv_heads, interpret=interpret)
    o = flash_call(q, k, v, kv_heads, interpret=interpret)
    return matmul_call(o[None], w_o[None], h[None], name="attn_out", interpret=interpret)[0]


def ffn_layer(h, norm_g, wg, wu, wd, interpret=False):
    u = rms_norm_call(h, norm_g, BF16, interpret=interpret)
    return swiglu_call(u, wg, wu, wd, h, interpret=interpret)


MOE_TILE = 512
TOP_K = 2


def moe_layer(h, norm_g, w_router, wg, wu, wd, interpret=False):
    T, D = h.shape
    n_experts = w_router.shape[1]
    tm = min(MOE_TILE, T)
    n_tiles = TOP_K * T // tm + n_experts
    u, info, counts = router_call(h, norm_g, w_router, interpret=interpret)
    pos, tile_expert, n_used = _routing_tables(info, counts, n_experts, tm, n_tiles)
    xs = dispatch_call(u, pos, n_tiles * tm, interpret=interpret)
    ys = experts_call(xs, tile_expert, n_used, wg, wu, wd, tm=tm, interpret=interpret)
    return combine_call(h, info, ys, pos, interpret=interpret)


GRID_W = 64


def kernel(x, norm_mix_g, norm_ffn_g, rwkv_mix, rwkv_w_rkv, rwkv_w0, rwkv_w1, rwkv_w2, rwkv_a0, rwkv_a1, rwkv_a2, rwkv_g1, rwkv_g2, rwkv_k_k, rwkv_k_a, rwkv_r_k, rwkv_ln_w, rwkv_ln_b, rwkv_w_out, attn_w_qkv, attn_q_norm, attn_k_norm, attn_w_o, ffn_w_gate, ffn_w_up, ffn_w_down, moe_router, moe_w_gate, moe_w_up, moe_w_down, final_norm_g):
    B, T, D = x.shape
    depth = norm_mix_g.shape[0]
    outs = []
    for bi in range(B):
        h = x[bi]
        for i in range(depth):
            j = i // 2
            if i % 2 == 0:
                h = rwkv_layer(h, norm_mix_g[i], rwkv_mix[j], rwkv_w_rkv[j], rwkv_w0[j], rwkv_w1[j], rwkv_w2[j],
                               rwkv_a0[j], rwkv_a1[j], rwkv_a2[j], rwkv_g1[j], rwkv_g2[j], rwkv_k_k[j],
                               rwkv_k_a[j], rwkv_r_k[j], rwkv_ln_w[j], rwkv_ln_b[j], rwkv_w_out[j])
                h = ffn_layer(h, norm_ffn_g[i], ffn_w_gate[j], ffn_w_up[j], ffn_w_down[j])
            else:
                h = attn_layer(h, norm_mix_g[i], attn_w_qkv[j], attn_q_norm[j], attn_k_norm[j], attn_w_o[j], GRID_W)
                h = moe_layer(h, norm_ffn_g[i], moe_router[j], moe_w_gate[j], moe_w_up[j], moe_w_down[j])
        outs.append(rms_norm_call(h, final_norm_g, x.dtype))
    return jnp.stack(outs)
```

```python
import functools
import math

import jax
import jax.numpy as jnp
from jax import lax
from jax.experimental import pallas as pl
from jax.experimental.pallas import tpu as pltpu

F32 = jnp.float32
BF16 = jnp.bfloat16
HIGHEST = lax.Precision.HIGHEST

LANES = 128
RWKV_HEAD = 64
WKV_CHUNK = 64
GN_EPS = 64e-5
NORM_EPS = 1e-6


def _dot(a, b):
    return jnp.dot(a.astype(BF16), b.astype(BF16), preferred_element_type=F32)


def _dot_nt(a, b):
    return lax.dot_general(a.astype(BF16), b.astype(BF16), (((1,), (1,)), ((), ())),
                           preferred_element_type=F32)


def _dot_tn(a, b):
    return lax.dot_general(a.astype(BF16), b.astype(BF16), (((0,), (0,)), ((), ())),
                           preferred_element_type=F32)


def _dot_f32(a, b):
    return jnp.dot(a, b, preferred_element_type=F32, precision=HIGHEST)


def _dot_split(x, m, pieces, exact_lhs=False):
    mb = m.astype(BF16)
    acc = None
    rem = x
    for _ in range(pieces):
        part = rem.astype(BF16)
        rem = rem - part.astype(F32)
        t = (jnp.dot(mb, part, preferred_element_type=F32) if exact_lhs
             else jnp.dot(part, mb, preferred_element_type=F32))
        acc = t if acc is None else acc + t
    return acc


def _wkv_kernel(r_ref, k_ref, v_ref, lw_ref, a_ref, kk_ref, ka_ref, rk_ref,
                y_ref, bonus_ref, s_ref, *, pairs):
    L = WKV_CHUNK
    d = pl.program_id(0)
    c = pl.program_id(2)

    @pl.when(c == 0)
    def _():
        s_ref[...] = jnp.zeros_like(s_ref)

    sign = 1 - 2 * d
    row = lax.broadcasted_iota(jnp.int32, (L, 2 * L), 0)
    col = lax.broadcasted_iota(jnp.int32, (L, 2 * L), 1)
    order = (row - (col & (L - 1))) * sign
    strict = order > 0
    incl = order >= 0
    first_head_cols = col < L
    r2 = lax.broadcasted_iota(jnp.int32, (L, L), 0)
    c2 = lax.broadcasted_iota(jnp.int32, (L, L), 1)
    tri = jnp.where((r2 - c2) * sign >= 0, 1.0, 0.0).astype(F32)
    lane = lax.broadcasted_iota(jnp.int32, (L, LANES), 1)
    first_head = lane < RWKV_HEAD
    hr = lax.broadcasted_iota(jnp.int32, (LANES, LANES), 0)
    hc = lax.broadcasted_iota(jnp.int32, (LANES, LANES), 1)
    head_ones = jnp.where((hr < RWKV_HEAD) == (hc < RWKV_HEAD), 1.0, 0.0).astype(F32)

    def stack(x):
        return jnp.concatenate([jnp.where(first_head, x, 0.0), jnp.where(first_head, 0.0, x)], axis=0)

    def block_diag(m):
        return jnp.concatenate([jnp.where(first_head_cols, m, 0.0), jnp.where(first_head_cols, 0.0, m)], axis=0)

    P = range(pairs)
    sls = [slice(p * LANES, (p + 1) * LANES) for p in P]
    each = lambda fn: [fn(p) for p in P]
    r = each(lambda p: r_ref[:, sls[p]])
    k = each(lambda p: k_ref[:, sls[p]])
    v = each(lambda p: v_ref[:, sls[p]])
    lw = each(lambda p: lw_ref[0, :, sls[p]])
    a = each(lambda p: a_ref[0, :, sls[p]])

    kkr = each(lambda p: k[p] * kk_ref[:, sls[p]])
    ss = each(lambda p: _dot_split(kkr[p] * kkr[p], head_ones, 2))
    cs = each(lambda p: _dot_split(lw[p], tri, 3, exact_lhs=True))
    kd = each(lambda p: k[p] * (1.0 + (a[p] - 1.0) * ka_ref[:, sls[p]]))
    bsum = each(lambda p: _dot_split(r[p] * kd[p] * rk_ref[:, sls[p]], head_ones, 2))
    for p in P:
        bonus_ref[0, :, sls[p]] = bsum[p] * v[p]
    kk = each(lambda p: kkr[p] / jnp.maximum(jnp.sqrt(ss[p]), 1e-12))
    b = each(lambda p: kk[p] * a[p])
    tot = each(lambda p: jnp.sum(lw[p], axis=0, keepdims=True))
    a_t = each(lambda p: -kk[p] * jnp.exp(cs[p] - lw[p]))
    r_t = each(lambda p: r[p] * jnp.exp(cs[p]))
    g_inv = each(lambda p: jnp.exp(-cs[p]))
    g_rem = each(lambda p: jnp.exp(tot[p] - cs[p]))

    sc = each(lambda p: _dot_nt(jnp.concatenate([a_t[p], r_t[p]], axis=0),
                                jnp.concatenate([stack(b[p] * g_inv[p]), stack(kd[p] * g_inv[p])], axis=0)))
    m_ak = each(lambda p: jnp.where(strict, sc[p][:L, 2 * L:], 0.0))
    m_rb = each(lambda p: jnp.where(incl, sc[p][L:, :2 * L], 0.0))
    m_rk = each(lambda p: jnp.where(incl, sc[p][L:, 2 * L:], 0.0))

    pw = each(lambda p: block_diag(jnp.where(strict, sc[p][:L, :2 * L], 0.0)))
    n = pw
    for _ in range(int(math.log2(L)) - 1):
        pw = each(lambda p: _dot(pw[p], pw[p]))
        n = each(lambda p: n[p] + pw[p] + _dot(n[p], pw[p]))

    s = each(lambda p: s_ref[p])
    vs = each(lambda p: stack(v[p]))
    x = each(lambda p: _dot_nt(a_t[p], s[p]) + _dot(m_ak[p], vs[p]))
    xs = each(lambda p: stack(x[p]))
    us = each(lambda p: xs[p] + _dot(n[p], xs[p]))
    for p in P:
        y_ref[0, :, sls[p]] = _dot_nt(r_t[p], s[p]) + _dot(m_rb[p], us[p]) + _dot(m_rk[p], vs[p])
    for p in P:
        s_ref[p] = s[p] * jnp.exp(tot[p]) + _dot_tn(
            jnp.concatenate([us[p], vs[p]], axis=0),
            jnp.concatenate([stack(b[p] * g_rem[p]), stack(kd[p] * g_rem[p])], axis=0))


def wkv7_bidir(rkv, logw, a, k_k, k_a, r_k, *, pairs=8, interpret=False):
    _, T, D = rkv.shape
    L = WKV_CHUNK
    nc = T // L
    pairs = min(pairs, D // LANES)
    W = LANES * pairs
    nj = D // W
    chunk = lambda d, c: c + d * (nc - 1 - 2 * c)
    dir_map = lambda d, j, c: (d, chunk(d, c), j)
    par_map = lambda d, j, c: (0, j)
    tok = lambda s: pl.BlockSpec((None, L, W), lambda d, j, c: (s, chunk(d, c), j))
    dtok = pl.BlockSpec((1, L, W), dir_map)
    par = pl.BlockSpec((1, W), par_map)
    return pl.pallas_call(
        functools.partial(_wkv_kernel, pairs=pairs),
        grid=(2, nj, nc),
        in_specs=[tok(0), tok(1), tok(2), dtok, dtok, par, par, par],
        out_specs=[dtok, dtok],
        out_shape=[jax.ShapeDtypeStruct((2, T, D), F32)] * 2,
        scratch_shapes=[pltpu.VMEM((pairs, LANES, LANES), F32)],
        compiler_params=pltpu.CompilerParams(dimension_semantics=("arbitrary", "arbitrary", "arbitrary")),
        name="wkv7_bidir",
        interpret=interpret,
    )(rkv, rkv, rkv, logw, a, k_k.reshape(1, D), k_a.reshape(1, D), r_k.reshape(1, D))


def _params(sem, vmem_mb):
    return pltpu.CompilerParams(dimension_semantics=sem, vmem_limit_bytes=vmem_mb * 1024 * 1024)


def _rms(x, g):
    return x * lax.rsqrt(jnp.mean(x * x, axis=-1, keepdims=True) + NORM_EPS) * g


def _norm_kernel(h_ref, g_ref, o_ref):
    o_ref[...] = _rms(h_ref[...], g_ref[...]).astype(o_ref.dtype)


def rms_norm_call(h, g, out_dtype, *, tm=256, interpret=False):
    T, D = h.shape
    tm = min(tm, T)
    return pl.pallas_call(
        _norm_kernel,
        grid=(T // tm,),
        in_specs=[pl.BlockSpec((tm, D), lambda i: (i, 0)), pl.BlockSpec((1, D), lambda i: (0, 0))],
        out_specs=pl.BlockSpec((tm, D), lambda i: (i, 0)),
        out_shape=jax.ShapeDtypeStruct((T, D), out_dtype),
        compiler_params=_params(("arbitrary",), 32),
        name="rms_norm",
        interpret=interpret,
    )(h, g.reshape(1, D))


HALO = 8


def _mixprep_kernel(h_ref, hp_ref, hn_ref, g_ref, mix_ref, o_ref):
    i = pl.program_id(0)
    last = pl.num_programs(0) - 1
    g = g_ref[...]
    u = _rms(h_ref[...], g)
    tm = u.shape[0]
    up = jnp.where(i == 0, 0.0, _rms(hp_ref[HALO - 1:HALO, :], g))
    un = jnp.where(i == last, 0.0, _rms(hn_ref[0:1, :], g))
    rows = lax.broadcasted_iota(jnp.int32, u.shape, 0)
    prev = jnp.where(rows == 0, up, pltpu.roll(u, 1, 0))
    nxt = jnp.where(rows == tm - 1, un, pltpu.roll(u, tm - 1, 0))
    xx = 0.5 * (prev + nxt) - u
    for m in range(o_ref.shape[0]):
        o_ref[m] = (u + xx * mix_ref[m:m + 1, :]).astype(o_ref.dtype)


def mixprep_call(h, g, mix, *, tm=256, interpret=False):
    T, D = h.shape
    tm = min(tm, T)
    nm = mix.shape[0]
    nb = tm // HALO
    nh = T // HALO
    return pl.pallas_call(
        _mixprep_kernel,
        grid=(T // tm,),
        in_specs=[pl.BlockSpec((tm, D), lambda i: (i, 0)),
                  pl.BlockSpec((HALO, D), lambda i: (jnp.maximum(i * nb - 1, 0), 0)),
                  pl.BlockSpec((HALO, D), lambda i: (jnp.minimum((i + 1) * nb, nh - 1), 0)),
                  pl.BlockSpec((1, D), lambda i: (0, 0)),
                  pl.BlockSpec((nm, D), lambda i: (0, 0))],
        out_specs=pl.BlockSpec((nm, tm, D), lambda i: (0, i, 0)),
        out_shape=jax.ShapeDtypeStruct((nm, T, D), BF16),
        compiler_params=_params(("arbitrary",), 48),
        name="rwkv_mixprep",
        interpret=interpret,
    )(h, h, h, g.reshape(1, D), mix)


def _mm_kernel(*refs, has_resid):
    if has_resid:
        x_ref, w_ref, r_ref, o_ref, wb_ref = refs
    else:
        x_ref, w_ref, o_ref, wb_ref = refs

    @pl.when(pl.program_id(2) == 0)
    def _():
        wb_ref[...] = w_ref[...].astype(BF16)

    acc = jnp.dot(x_ref[...], wb_ref[...], preferred_element_type=F32)
    if has_resid:
        acc = acc + r_ref[...]
    o_ref[...] = acc.astype(o_ref.dtype)


def matmul_call(x, w, resid=None, *, out_dtype=F32, tm=512, tn=512, name="matmul", interpret=False):
    _, T, Kd = x.shape
    S, _, N = w.shape
    tm, tn = min(tm, T), min(tn, N)
    in_specs = [pl.BlockSpec((None, tm, Kd), lambda s, j, i: (s, i, 0)),
                pl.BlockSpec((None, Kd, tn), lambda s, j, i: (s, 0, j))]
    args = [x, w]
    if resid is not None:
        in_specs.append(pl.BlockSpec((None, tm, tn), lambda s, j, i: (s, i, j)))
        args.append(resid)
    return pl.pallas_call(
        functools.partial(_mm_kernel, has_resid=resid is not None),
        grid=(S, N // tn, T // tm),
        in_specs=in_specs,
        out_specs=pl.BlockSpec((None, tm, tn), lambda s, j, i: (s, i, j)),
        out_shape=jax.ShapeDtypeStruct((S, T, N), out_dtype),
        scratch_shapes=[pltpu.VMEM((Kd, tn), BF16)],
        compiler_params=_params(("arbitrary", "arbitrary", "arbitrary"), 48),
        name=name,
        interpret=interpret,
    )(*args)


def _lora_kernel(xw_ref, xa_ref, xg_ref, w1_ref, w2_ref, w0_ref, a1_ref, a2_ref, a0_ref, g1_ref, g2_ref,
                 g_ref, lw_ref, a_ref):
    xw = xw_ref[...]
    xa = xa_ref[...]
    g_ref[...] = _dot(jax.nn.sigmoid(_dot(xg_ref[...], g1_ref[...])), g2_ref[...])
    for d in range(2):
        wl = w0_ref[d:d + 1, :] + _dot(jnp.tanh(_dot(xw, w1_ref[d])), w2_ref[d])
        lw_ref[d] = -jnp.exp(-jax.nn.softplus(-wl) - 0.5)
        a_ref[d] = jax.nn.sigmoid(a0_ref[d:d + 1, :] + _dot(_dot(xa, a1_ref[d]), a2_ref[d]))


def _pad_to(x, axis, mult):
    pad = (-x.shape[axis]) % mult
    if pad == 0:
        return x
    widths = [(0, 0)] * x.ndim
    widths[axis] = (0, pad)
    return jnp.pad(x, widths)


def lora_call(xs, w0, w1, w2, a0, a1, a2, g1, g2, *, tm=256, interpret=False):
    _, T, D = xs.shape
    tm = min(tm, T)
    w1 = _pad_to(w1, 2, LANES).astype(BF16)
    w2 = _pad_to(w2, 1, LANES).astype(BF16)
    a1 = _pad_to(a1, 2, LANES).astype(BF16)
    a2 = _pad_to(a2, 1, LANES).astype(BF16)
    g1 = g1.astype(BF16)
    g2 = g2.astype(BF16)
    xspec = lambda m: pl.BlockSpec((None, tm, D), lambda i: (m, i, 0))
    full = lambda a: pl.BlockSpec(a.shape, lambda i: (0,) * a.ndim)
    return pl.pallas_call(
        _lora_kernel,
        grid=(T // tm,),
        in_specs=[xspec(3), xspec(4), xspec(5), full(w1), full(w2), full(w0), full(a1), full(a2), full(a0),
                  full(g1), full(g2)],
        out_specs=[pl.BlockSpec((tm, D), lambda i: (i, 0)),
                   pl.BlockSpec((2, tm, D), lambda i: (0, i, 0)),
                   pl.BlockSpec((2, tm, D), lambda i: (0, i, 0))],
        out_shape=[jax.ShapeDtypeStruct((T, D), F32), jax.ShapeDtypeStruct((2, T, D), F32),
                   jax.ShapeDtypeStruct((2, T, D), F32)],
        compiler_params=_params(("arbitrary",), 56),
        name="rwkv_lora",
        interpret=interpret,
    )(xs, xs, xs, w1, w2, w0, a1, a2, a0, g1, g2)


def _wkv_post_kernel(y_ref, b_ref, g_ref, lnw_ref, lnb_ref, o_ref):
    hr = lax.broadcasted_iota(jnp.int32, (LANES, LANES), 0)
    hc = lax.broadcasted_iota(jnp.int32, (LANES, LANES), 1)
    head_mean = jnp.where((hr < RWKV_HEAD) == (hc < RWKV_HEAD), 1.0 / RWKV_HEAD, 0.0).astype(F32)
    for p in range(o_ref.shape[1] // LANES):
        sl = slice(p * LANES, (p + 1) * LANES)
        y = y_ref[0, :, sl] + y_ref[1, :, sl]
        yc = y - _dot_f32(y, head_mean)
        var = _dot_f32(yc * yc, head_mean)
        yn = yc * lax.rsqrt(var + GN_EPS)
        out = yn * lnw_ref[:, sl] + lnb_ref[:, sl] + b_ref[0, :, sl] + b_ref[1, :, sl]
        o_ref[:, sl] = (out * g_ref[:, sl]).astype(o_ref.dtype)


def wkv_post_call(y, bonus, g, ln_w, ln_b, *, tm=256, interpret=False):
    _, T, D = y.shape
    tm = min(tm, T)
    two = pl.BlockSpec((2, tm, D), lambda i: (0, i, 0))
    one = pl.BlockSpec((tm, D), lambda i: (i, 0))
    par = pl.BlockSpec((1, D), lambda i: (0, 0))
    return pl.pallas_call(
        _wkv_post_kernel,
        grid=(T // tm,),
        in_specs=[two, two, one, par, par],
        out_specs=one,
        out_shape=jax.ShapeDtypeStruct((T, D), BF16),
        compiler_params=_params(("arbitrary",), 48),
        name="wkv_post",
        interpret=interpret,
    )(y, bonus, g, ln_w.reshape(1, D), ln_b.reshape(1, D))


def _swiglu_kernel(x_ref, wg_ref, wu_ref, wd_ref, r_ref, o_ref):
    @pl.when(pl.program_id(1) == 0)
    def _():
        o_ref[...] = r_ref[...]

    x = x_ref[...]
    hg = jnp.dot(x, wg_ref[...].astype(BF16), preferred_element_type=F32)
    hu = jnp.dot(x, wu_ref[...].astype(BF16), preferred_element_type=F32)
    h = hg * jax.nn.sigmoid(hg) * hu
    o_ref[...] += jnp.dot(h.astype(BF16), wd_ref[...].astype(BF16), preferred_element_type=F32)


def swiglu_call(x, wg, wu, wd, resid, *, tm=512, tf=512, interpret=False):
    T, D = x.shape
    Fd = wg.shape[1]
    tm, tf = min(tm, T), min(tf, Fd)
    return pl.pallas_call(
        _swiglu_kernel,
        grid=(T // tm, Fd // tf),
        in_specs=[pl.BlockSpec((tm, D), lambda i, f: (i, 0)),
                  pl.BlockSpec((D, tf), lambda i, f: (0, f)),
                  pl.BlockSpec((D, tf), lambda i, f: (0, f)),
                  pl.BlockSpec((tf, D), lambda i, f: (f, 0)),
                  pl.BlockSpec((tm, D), lambda i, f: (i, 0))],
        out_specs=pl.BlockSpec((tm, D), lambda i, f: (i, 0)),
        out_shape=jax.ShapeDtypeStruct((T, D), F32),
        compiler_params=_params(("arbitrary", "arbitrary"), 56),
        name="swiglu",
        interpret=interpret,
    )(x, wg, wu, wd, resid)


ATT_HEAD = 128
ROPE_HALF = 32


def _qkprep_kernel(qkv_ref, qn_ref, kn_ref, cos_ref, sin_ref, q_ref, k_ref, v_ref, *, q_heads, kv_heads):
    cos = cos_ref[...]
    sin = sin_ref[...]
    lane = lax.broadcasted_iota(jnp.int32, cos.shape, 1)
    low = (lane & (2 * ROPE_HALF - 1)) < ROPE_HALF
    scale = math.log2(math.e) / math.sqrt(ATT_HEAD)

    def head(x, g):
        xn = _rms(x, g)
        partner = jnp.where(low, pltpu.roll(xn, ATT_HEAD - ROPE_HALF, 1), pltpu.roll(xn, ROPE_HALF, 1))
        return xn * cos + partner * sin

    for hh in range(q_heads):
        sl = slice(hh * ATT_HEAD, (hh + 1) * ATT_HEAD)
        q_ref[:, sl] = (head(qkv_ref[:, sl], qn_ref[...]) * scale).astype(q_ref.dtype)
    for hh in range(kv_heads):
        src = slice((q_heads + hh) * ATT_HEAD, (q_heads + hh + 1) * ATT_HEAD)
        dst = slice(hh * ATT_HEAD, (hh + 1) * ATT_HEAD)
        k_ref[:, dst] = head(qkv_ref[:, src], kn_ref[...]).astype(k_ref.dtype)
    v0 = (q_heads + kv_heads) * ATT_HEAD
    v_ref[...] = qkv_ref[:, v0:].astype(v_ref.dtype)


def qkprep_call(qkv, q_norm, k_norm, cos, sin, q_heads, kv_heads, *, tm=256, interpret=False):
    T, Wd = qkv.shape
    tm = min(tm, T)
    qw, kw = q_heads * ATT_HEAD, kv_heads * ATT_HEAD
    row = lambda w: pl.BlockSpec((tm, w), lambda i: (i, 0))
    par = pl.BlockSpec((1, ATT_HEAD), lambda i: (0, 0))
    return pl.pallas_call(
        functools.partial(_qkprep_kernel, q_heads=q_heads, kv_heads=kv_heads),
        grid=(T // tm,),
        in_specs=[row(Wd), par, par, row(ATT_HEAD), row(ATT_HEAD)],
        out_specs=[row(qw), row(kw), row(kw)],
        out_shape=[jax.ShapeDtypeStruct((T, qw), BF16), jax.ShapeDtypeStruct((T, kw), BF16),
                   jax.ShapeDtypeStruct((T, kw), BF16)],
        compiler_params=_params(("arbitrary",), 32),
        name="attn_qkprep",
        interpret=interpret,
    )(qkv, q_norm.reshape(1, ATT_HEAD), k_norm.reshape(1, ATT_HEAD), cos, sin)


def _rope_tables(T, grid_w):
    pos = jnp.arange(T)
    row_ids = (pos // grid_w).astype(F32)
    col_ids = (pos % grid_w).astype(F32)
    axis_dim = ATT_HEAD // 2
    inv = 10000.0 ** (-jnp.arange(0, axis_dim, 2, dtype=F32) / axis_dim)
    ang_r = row_ids[:, None] * inv[None, :]
    ang_c = col_ids[:, None] * inv[None, :]
    cos = jnp.concatenate([jnp.cos(ang_r), jnp.cos(ang_r), jnp.cos(ang_c), jnp.cos(ang_c)], axis=-1)
    sin = jnp.concatenate([-jnp.sin(ang_r), jnp.sin(ang_r), -jnp.sin(ang_c), jnp.sin(ang_c)], axis=-1)
    return cos, sin


def _flash_kernel(q_ref, k_ref, v_ref, o_ref, m_ref, l_ref, acc_ref, *, rep):
    kv = pl.program_id(2)

    @pl.when(kv == 0)
    def _():
        m_ref[...] = jnp.full_like(m_ref, -jnp.inf)
        l_ref[...] = jnp.zeros_like(l_ref)
        acc_ref[...] = jnp.zeros_like(acc_ref)

    k = k_ref[...]
    v = v_ref[...]
    for hh in range(rep):
        sl = slice(hh * ATT_HEAD, (hh + 1) * ATT_HEAD)
        s = lax.dot_general(q_ref[:, sl], k, (((1,), (1,)), ((), ())), preferred_element_type=F32)
        m_prev = m_ref[hh]
        m_new = jnp.maximum(m_prev, jnp.max(s, axis=-1, keepdims=True))
        alpha = jnp.exp2(m_prev - m_new)
        p = jnp.exp2(s - m_new[:, :1])
        l_ref[hh] = alpha * l_ref[hh] + jnp.sum(p, axis=-1, keepdims=True)
        m_ref[hh] = m_new
        acc_ref[:, sl] = alpha * acc_ref[:, sl] + jnp.dot(p.astype(BF16), v, preferred_element_type=F32)

    @pl.when(kv == pl.num_programs(2) - 1)
    def _():
        for hh in range(rep):
            sl = slice(hh * ATT_HEAD, (hh + 1) * ATT_HEAD)
            o_ref[:, sl] = (acc_ref[:, sl] / l_ref[hh]).astype(o_ref.dtype)


def flash_call(q, k, v, kv_heads, *, tq=512, tk=8192, interpret=False):
    T, qw = q.shape
    rep = qw // ATT_HEAD // kv_heads
    tq, tk = min(tq, T), min(tk, T)
    gw = rep * ATT_HEAD
    return pl.pallas_call(
        functools.partial(_flash_kernel, rep=rep),
        grid=(kv_heads, T // tq, T // tk),
        in_specs=[pl.BlockSpec((tq, gw), lambda g, i, j: (i, g)),
                  pl.BlockSpec((tk, ATT_HEAD), lambda g, i, j: (j, g)),
                  pl.BlockSpec((tk, ATT_HEAD), lambda g, i, j: (j, g))],
        out_specs=pl.BlockSpec((tq, gw), lambda g, i, j: (i, g)),
        out_shape=jax.ShapeDtypeStruct((T, qw), BF16),
        scratch_shapes=[pltpu.VMEM((rep, tq, ATT_HEAD), F32), pltpu.VMEM((rep, tq, ATT_HEAD), F32),
                        pltpu.VMEM((tq, gw), F32)],
        compiler_params=_params(("arbitrary", "arbitrary", "arbitrary"), 48),
        name="gqa_flash",
        interpret=interpret,
    )(q, k, v)


(INFO_E1, INFO_E2, INFO_RANK1, INFO_RANK2, INFO_P1, INFO_P2) = range(6)


def _router_kernel(h_ref, g_ref, wr_ref, u_ref, info_ref, cnt_ref, *, n_experts):
    @pl.when(pl.program_id(0) == 0)
    def _():
        cnt_ref[...] = jnp.zeros_like(cnt_ref)

    u = _rms(h_ref[...], g_ref[...])
    u_ref[...] = u
    logits = _dot_f32(u, wr_ref[...])
    lane = lax.broadcasted_iota(jnp.int32, logits.shape, 1)
    logits = jnp.where(lane < n_experts, logits, -jnp.inf)
    m1 = jnp.max(logits, axis=-1, keepdims=True)
    i1 = jnp.min(jnp.where(logits == m1, lane, LANES), axis=-1, keepdims=True)
    rest = jnp.where(lane == i1, -jnp.inf, logits)
    m2 = jnp.max(rest, axis=-1, keepdims=True)
    i2 = jnp.min(jnp.where(rest == m2, lane, LANES), axis=-1, keepdims=True)
    e2 = jnp.exp(m2 - m1)
    p1 = 1.0 / (1.0 + e2)
    p2 = e2 / (1.0 + e2)

    oh1 = jnp.where(lane == i1, 1.0, 0.0)
    oh2 = jnp.where(lane == i2, 1.0, 0.0)
    both = oh1 + oh2
    tm = both.shape[0]
    tr = lax.broadcasted_iota(jnp.int32, (tm, tm), 0)
    tc = lax.broadcasted_iota(jnp.int32, (tm, tm), 1)
    cum = _dot(jnp.where(tr >= tc, 1.0, 0.0), both)
    before = cum - both + cnt_ref[...]
    rank1 = jnp.sum(oh1 * before, axis=-1, keepdims=True)
    rank2 = jnp.sum(oh2 * before, axis=-1, keepdims=True)
    cnt_ref[...] += cum[tm - 1:tm, :]

    info = jnp.zeros_like(logits)
    for idx, val in ((INFO_E1, i1.astype(F32)), (INFO_E2, i2.astype(F32)), (INFO_RANK1, rank1),
                     (INFO_RANK2, rank2), (INFO_P1, p1), (INFO_P2, p2)):
        info = jnp.where(lane == idx, val, info)
    info_ref[...] = info


def router_call(h, g, w_router, *, tm=256, interpret=False):
    T, D = h.shape
    tm = min(tm, T)
    n_experts = w_router.shape[1]
    wr = _pad_to(w_router, 1, LANES)
    return pl.pallas_call(
        functools.partial(_router_kernel, n_experts=n_experts),
        grid=(T // tm,),
        in_specs=[pl.BlockSpec((tm, D), lambda i: (i, 0)), pl.BlockSpec((1, D), lambda i: (0, 0)),
                  pl.BlockSpec((D, LANES), lambda i: (0, 0))],
        out_specs=[pl.BlockSpec((tm, D), lambda i: (i, 0)), pl.BlockSpec((tm, LANES), lambda i: (i, 0)),
                   pl.BlockSpec((1, LANES), lambda i: (0, 0))],
        out_shape=[jax.ShapeDtypeStruct((T, D), F32), jax.ShapeDtypeStruct((T, LANES), F32),
                   jax.ShapeDtypeStruct((1, LANES), F32)],
        compiler_params=_params(("arbitrary",), 32),
        name="moe_router",
        interpret=interpret,
    )(h, g.reshape(1, D), wr)


def _dispatch_kernel(pos_ref, u_ref, xs_init_hbm, xs_hbm, sem, *, tm, n_tok):
    del xs_init_hbm
    base = pl.program_id(0) * tm

    def row_copies(r):
        src = u_ref.at[pl.ds(r, 1)]
        t = base + r
        return (pltpu.make_async_copy(src, xs_hbm.at[pl.ds(pos_ref[t], 1)], sem),
                pltpu.make_async_copy(src, xs_hbm.at[pl.ds(pos_ref[n_tok + t], 1)], sem))

    def issue(r, carry):
        for cp in row_copies(r):
            cp.start()
        return carry

    def drain(r, carry):
        for cp in row_copies(r):
            cp.wait()
        return carry

    lax.fori_loop(0, tm, issue, 0)
    lax.fori_loop(0, tm, drain, 0)


def dispatch_call(u, pos, n_rows, *, tm=256, interpret=False):
    T, D = u.shape
    tm = min(tm, T)
    xs0 = jnp.zeros((n_rows, D), u.dtype)
    return pl.pallas_call(
        functools.partial(_dispatch_kernel, tm=tm, n_tok=T),
        grid_spec=pltpu.PrefetchScalarGridSpec(
            num_scalar_prefetch=1,
            grid=(T // tm,),
            in_specs=[pl.BlockSpec((tm, D), lambda i, pos: (i, 0)), pl.BlockSpec(memory_space=pl.ANY)],
            out_specs=pl.BlockSpec(memory_space=pl.ANY),
            scratch_shapes=[pltpu.SemaphoreType.DMA(())]),
        out_shape=jax.ShapeDtypeStruct((n_rows, D), u.dtype),
        input_output_aliases={2: 0},
        compiler_params=_params(("arbitrary",), 32),
        name="moe_dispatch",
        interpret=interpret,
    )(pos, u, xs0)


def _experts_kernel(te_ref, nused_ref, x_ref, wg_ref, wu_ref, wd_ref, o_ref, xb_ref):
    del te_ref
    i = pl.program_id(0)
    f = pl.program_id(1)

    @pl.when(i < nused_ref[0])
    def _():
        @pl.when(f == 0)
        def _():
            xb_ref[...] = x_ref[...].astype(BF16)
            o_ref[...] = jnp.zeros_like(o_ref)

        x = xb_ref[...]
        hg = jnp.dot(x, wg_ref[...].astype(BF16), preferred_element_type=F32)
        hu = jnp.dot(x, wu_ref[...].astype(BF16), preferred_element_type=F32)
        h = hg * jax.nn.sigmoid(hg) * hu
        o_ref[...] += jnp.dot(h.astype(BF16), wd_ref[...].astype(BF16), preferred_element_type=F32)

    @pl.when((i >= nused_ref[0]) & (f == 0))
    def _():
        o_ref[...] = jnp.zeros_like(o_ref)


def experts_call(xs, tile_expert, n_used, wg, wu, wd, *, tm, tf=256, interpret=False):
    n_rows, D = xs.shape
    E, _, Fd = wg.shape
    tf = min(tf, Fd)
    nf = Fd // tf
    nt = n_rows // tm

    def tile(i, nu):
        return jnp.maximum(jnp.minimum(i, nu[0] - 1), 0)

    def ftile(i, f, nu):
        return jnp.where(i < nu[0], f, nf - 1)

    return pl.pallas_call(
        _experts_kernel,
        grid_spec=pltpu.PrefetchScalarGridSpec(
            num_scalar_prefetch=2,
            grid=(nt, nf),
            in_specs=[pl.BlockSpec((tm, D), lambda i, f, te, nu: (tile(i, nu), 0)),
                      pl.BlockSpec((None, D, tf), lambda i, f, te, nu: (te[tile(i, nu)], 0, ftile(i, f, nu))),
                      pl.BlockSpec((None, D, tf), lambda i, f, te, nu: (te[tile(i, nu)], 0, ftile(i, f, nu))),
                      pl.BlockSpec((None, tf, D), lambda i, f, te, nu: (te[tile(i, nu)], ftile(i, f, nu), 0))],
            out_specs=pl.BlockSpec((tm, D), lambda i, f, te, nu: (i, 0)),
            scratch_shapes=[pltpu.VMEM((tm, D), BF16)]),
        out_shape=jax.ShapeDtypeStruct((n_rows, D), F32),
        compiler_params=_params(("arbitrary", "arbitrary"), 56),
        name="moe_experts",
        interpret=interpret,
    )(tile_expert, n_used, xs, wg, wu, wd)


def _combine_kernel(pos_ref, h_ref, info_ref, y_hbm, *rest, tm, n_tok, final_norm):
    if final_norm:
        g_ref, o_ref, buf_ref, sem = rest
    else:
        o_ref, buf_ref, sem = rest
    i = pl.program_id(0)
    n = pl.num_programs(0)

    def row_copies(tile, slot, r):
        t = tile * tm + r
        return (pltpu.make_async_copy(y_hbm.at[pl.ds(pos_ref[t], 1)], buf_ref.at[slot, 0, pl.ds(r, 1)],
                                      sem.at[slot]),
                pltpu.make_async_copy(y_hbm.at[pl.ds(pos_ref[n_tok + t], 1)], buf_ref.at[slot, 1, pl.ds(r, 1)],
                                      sem.at[slot]))

    def issue(tile, slot):
        def body(r, carry):
            for cp in row_copies(tile, slot, r):
                cp.start()
            return carry
        lax.fori_loop(0, tm, body, 0)

    @pl.when(i == 0)
    def _():
        issue(0, 0)

    @pl.when(i + 1 < n)
    def _():
        issue(i + 1, (i + 1) % 2)

    slot = i % 2

    def drain(r, carry):
        for cp in row_copies(i, slot, r):
            cp.wait()
        return carry
    lax.fori_loop(0, tm, drain, 0)

    info = info_ref[...]
    lane = lax.broadcasted_iota(jnp.int32, info.shape, 1)
    p1 = jnp.sum(jnp.where(lane == INFO_P1, info, 0.0), axis=-1, keepdims=True)
    p2 = jnp.sum(jnp.where(lane == INFO_P2, info, 0.0), axis=-1, keepdims=True)
    out = h_ref[...] + p1 * buf_ref[slot, 0] + p2 * buf_ref[slot, 1]
    if final_norm:
        out = _rms(out, g_ref[...])
    o_ref[...] = out


def combine_call(h, info, y, pos, final_g=None, *, tm=256, interpret=False):
    T, D = h.shape
    tm = min(tm, T)
    in_specs = [pl.BlockSpec((tm, D), lambda i, pos: (i, 0)),
                pl.BlockSpec((tm, LANES), lambda i, pos: (i, 0)),
                pl.BlockSpec(memory_space=pl.ANY)]
    args = [pos, h, info, y]
    if final_g is not None:
        in_specs.append(pl.BlockSpec((1, D), lambda i, pos: (0, 0)))
        args.append(final_g.reshape(1, D))
    return pl.pallas_call(
        functools.partial(_combine_kernel, tm=tm, n_tok=T, final_norm=final_g is not None),
        grid_spec=pltpu.PrefetchScalarGridSpec(
            num_scalar_prefetch=1,
            grid=(T // tm,),
            in_specs=in_specs,
            out_specs=pl.BlockSpec((tm, D), lambda i, pos: (i, 0)),
            scratch_shapes=[pltpu.VMEM((2, 2, tm, D), F32), pltpu.SemaphoreType.DMA((2,))]),
        out_shape=jax.ShapeDtypeStruct((T, D), F32),
        compiler_params=_params(("arbitrary",), 40),
        name="moe_combine",
        interpret=interpret,
    )(*args)


def _routing_tables(info, counts, n_experts, tm, n_tiles):
    cnt = counts[0, :n_experts].astype(jnp.int32)
    tiles_e = (cnt + tm - 1) // tm
    cum_tiles = jnp.cumsum(tiles_e)
    row_off = (cum_tiles - tiles_e) * tm
    experts = jnp.arange(n_experts, dtype=jnp.int32)

    def slot(e_lane, r_lane):
        e = info[:, e_lane].astype(jnp.int32)
        off = jnp.sum(jnp.where(e[:, None] == experts[None, :], row_off[None, :], 0), axis=-1)
        return off + info[:, r_lane].astype(jnp.int32)

    pos = jnp.concatenate([slot(INFO_E1, INFO_RANK1), slot(INFO_E2, INFO_RANK2)])
    tiles = jnp.arange(n_tiles, dtype=jnp.int32)
    tile_expert = jnp.minimum(jnp.sum(tiles[:, None] >= cum_tiles[None, :], axis=-1), n_experts - 1)
    return pos, tile_expert.astype(jnp.int32), cum_tiles[-1:].astype(jnp.int32)


def rwkv_layer(h, norm_g, mix, w_rkv, w0, w1, w2, a0, a1, a2, g1, g2, k_k, k_a, r_k, ln_w, ln_b, w_out,
               interpret=False):
    T, D = h.shape
    order = jnp.array([0, 2, 3, 1, 4, 5])
    xs = mixprep_call(h, norm_g, mix[order], interpret=interpret)
    rkv = matmul_call(xs, w_rkv, name="rwkv_rkv", interpret=interpret)
    g, logw, a = lora_call(xs, w0, w1, w2, a0, a1, a2, g1, g2, interpret=interpret)
    y, bonus = wkv7_bidir(rkv, logw, a, k_k, k_a, r_k.reshape(D), interpret=interpret)
    z = wkv_post_call(y, bonus, g, ln_w, ln_b, interpret=interpret)
    return matmul_call(z[None], w_out[None], h[None], name="rwkv_out", interpret=interpret)[0]


def attn_layer(h, norm_g, w_qkv, q_norm, k_norm, w_o, grid_w, interpret=False):
    T, D = h.shape
    q_heads = w_o.shape[0] // ATT_HEAD
    kv_heads = (w_qkv.shape[1] // ATT_HEAD - q_heads) // 2
    u = rms_norm_call(h, norm_g, BF16, interpret=interpret)
    qkv = matmul_call(u[None], w_qkv[None], name="attn_qkv", interpret=interpret)[0]
    cos, sin = _rope_tables(T, grid_w)
    q, k, v = qkprep_call(qkv, q_norm, k_norm, cos, sin, q_heads, kv_heads, interpret=interpret)
    o = flash_call(q, k, v, kv_heads, interpret=interpret)
    return matmul_call(o[None], w_o[None], h[None], name="attn_out", interpret=interpret)[0]


def ffn_layer(h, norm_g, wg, wu, wd, interpret=False):
    u = rms_norm_call(h, norm_g, BF16, interpret=interpret)
    return swiglu_call(u, wg, wu, wd, h, interpret=interpret)


MOE_TILE = 1024
TOP_K = 2


def moe_layer(h, norm_g, w_router, wg, wu, wd, final_g=None, interpret=False):
    T, D = h.shape
    n_experts = w_router.shape[1]
    tm = min(MOE_TILE, T)
    n_tiles = TOP_K * T // tm + n_experts
    u, info, counts = router_call(h, norm_g, w_router, interpret=interpret)
    pos, tile_expert, n_used = _routing_tables(info, counts, n_experts, tm, n_tiles)
    xs = dispatch_call(u, pos, n_tiles * tm, interpret=interpret)
    ys = experts_call(xs, tile_expert, n_used, wg, wu, wd, tm=tm, interpret=interpret)
    return combine_call(h, info, ys, pos, final_g, interpret=interpret)


GRID_W = 64


def kernel(x, norm_mix_g, norm_ffn_g, rwkv_mix, rwkv_w_rkv, rwkv_w0, rwkv_w1, rwkv_w2, rwkv_a0, rwkv_a1, rwkv_a2, rwkv_g1, rwkv_g2, rwkv_k_k, rwkv_k_a, rwkv_r_k, rwkv_ln_w, rwkv_ln_b, rwkv_w_out, attn_w_qkv, attn_q_norm, attn_k_norm, attn_w_o, ffn_w_gate, ffn_w_up, ffn_w_down, moe_router, moe_w_gate, moe_w_up, moe_w_down, final_norm_g):
    B, T, D = x.shape
    depth = norm_mix_g.shape[0]
    outs = []
    for bi in range(B):
        h = x[bi]
        for i in range(depth):
            j = i // 2
            if i % 2 == 0:
                h = rwkv_layer(h, norm_mix_g[i], rwkv_mix[j], rwkv_w_rkv[j], rwkv_w0[j], rwkv_w1[j], rwkv_w2[j],
                               rwkv_a0[j], rwkv_a1[j], rwkv_a2[j], rwkv_g1[j], rwkv_g2[j], rwkv_k_k[j],
                               rwkv_k_a[j], rwkv_r_k[j], rwkv_ln_w[j], rwkv_ln_b[j], rwkv_w_out[j])
                h = ffn_layer(h, norm_ffn_g[i], ffn_w_gate[j], ffn_w_up[j], ffn_w_down[j])
            else:
                h = attn_layer(h, norm_mix_g[i], attn_w_qkv[j], attn_q_norm[j], attn_k_norm[j], attn_w_o[j], GRID_W)
                closing = final_norm_g if i == depth - 1 else None
                h = moe_layer(h, norm_ffn_g[i], moe_router[j], moe_w_gate[j], moe_w_up[j], moe_w_down[j], closing)
        outs.append(h if depth % 2 == 0 else rms_norm_call(h, final_norm_g, x.dtype))
    return jnp.stack(outs)
```
